```python
import math
import jax
import jax.numpy as jnp
from jax import lax
import numpy as np


D_MODEL = 1024
BATCH = 8
SEQ = 2048
DEPTH = 1

MEM_TOKENS = 256
GRID_W = 64
Q_BLOCK = 128
ROPE_THETA = 10000.0

A_HEADS = 8
A_KV_HEADS = 2
A_GROUP = A_HEADS // A_KV_HEADS
A_HEAD_DIM = D_MODEL // 16
D_A = A_HEADS * A_HEAD_DIM

B_HEADS = 4
B_HEAD_DIM = D_MODEL // 16
B_VDIM = 2 * B_HEAD_DIM
D_B = B_HEADS * B_VDIM

D_MIX = D_A + D_B

A_Q_COLS = A_HEADS * A_HEAD_DIM
A_KV_COLS = A_KV_HEADS * A_HEAD_DIM
B_QK_COLS = B_HEADS * 2 * B_HEAD_DIM
B_V_COLS = B_HEADS * B_VDIM
D_IN_PROJ = A_Q_COLS + 2 * A_KV_COLS + 2 * B_QK_COLS + B_V_COLS
IN_SPLITS = (A_Q_COLS,
             A_Q_COLS + A_KV_COLS,
             A_Q_COLS + 2 * A_KV_COLS,
             A_Q_COLS + 2 * A_KV_COLS + B_QK_COLS,
             A_Q_COLS + 2 * A_KV_COLS + 2 * B_QK_COLS)

MEM_HEADS = 4
MEM_HEAD_DIM = D_MODEL // MEM_HEADS

N_EXPERTS = 32
TOP_K = 4
D_FF = D_MODEL
SWIGLU_LIMIT = 7.0
SWIGLU_ALPHA = 1.702

DEEPNORM_ALPHA = (2.0 * DEPTH) ** 0.25
DEEPNORM_BETA = (8.0 * DEPTH) ** -0.25
LN_EPS = 1e-5
RMS_EPS = 1e-6

kernel_name = 'hybrid_gqa_axial_diffattn_memxattn_moe_deepnorm'


def layer_norm(x, g, b):
    xf = x.astype(jnp.float32)
    mu = jnp.mean(xf, axis=-1, keepdims=True)
    xc = xf - mu
    var = jnp.mean(xc * xc, axis=-1, keepdims=True)
    y = xc * lax.rsqrt(var + LN_EPS) * g.astype(jnp.float32) + b.astype(jnp.float32)
    return y.astype(x.dtype)


def rms_norm(x, g):
    xf = x.astype(jnp.float32)
    y = xf * lax.rsqrt(jnp.mean(xf * xf, axis=-1, keepdims=True) + RMS_EPS) * g.astype(jnp.float32)
    return y.astype(x.dtype)


def rope_cos_sin(pos, dim):
    inv = ROPE_THETA ** (-jnp.arange(0, dim, 2, dtype=jnp.float32) / dim)
    ang = pos.astype(jnp.float32)[:, None] * inv[None, :]
    return jnp.cos(ang), jnp.sin(ang)


def apply_rope(x, cos, sin):
    xf = x.astype(jnp.float32)
    x1, x2 = jnp.split(xf, 2, axis=-1)
    return jnp.concatenate([x1 * cos - x2 * sin, x2 * cos + x1 * sin], axis=-1).astype(x.dtype)


def axial_rope(x, cs_row, cs_col):
    half = x.shape[-1] // 2
    return jnp.concatenate([apply_rope(x[..., :half], *cs_row),
                            apply_rope(x[..., half:], *cs_col)], axis=-1)


def sweep_query_blocks(block_fn, q):
    S = q.shape[-2]
    nblk = S // Q_BLOCK
    qb = q.reshape(q.shape[:-2] + (nblk, Q_BLOCK, q.shape[-1]))
    qb = jnp.moveaxis(qb, -3, 0)
    out = lax.map(block_fn, qb)
    out = jnp.moveaxis(out, 0, -3)
    return out.reshape(out.shape[:-3] + (S, out.shape[-1]))


def gqa_axial_attention(q, k, v, q_g, k_g, cs_row, cs_col):
    B, S = q.shape[0], q.shape[1]
    q = axial_rope(rms_norm(q, q_g).transpose(0, 2, 1, 3), cs_row, cs_col)
    k = axial_rope(rms_norm(k, k_g).transpose(0, 2, 1, 3), cs_row, cs_col)
    v = v.transpose(0, 2, 1, 3)
    q = q.reshape(B, A_KV_HEADS, A_GROUP, S, A_HEAD_DIM)
    scale = A_HEAD_DIM ** -0.5

    def block(qb):
        s = jnp.einsum('bkgqd,bksd->bkgqs', qb, k).astype(jnp.float32) * scale
        p = jax.nn.softmax(s, axis=-1).astype(v.dtype)
        return jnp.einsum('bkgqs,bksd->bkgqd', p, v)

    o = sweep_query_blocks(block, q)
    return o.reshape(B, A_HEADS, S, A_HEAD_DIM).transpose(0, 2, 1, 3).reshape(B, S, D_A)


def differential_attention(q, k, v, lq1, lk1, lq2, lk2, subln_g, cs, lambda_init):
    B, S = q.shape[0], q.shape[1]
    q = apply_rope(q.transpose(0, 2, 3, 1, 4), *cs)
    k = apply_rope(k.transpose(0, 2, 3, 1, 4), *cs)
    v = v.transpose(0, 2, 1, 3)
    lam = (jnp.exp(jnp.sum(lq1.astype(jnp.float32) * lk1.astype(jnp.float32)))
           - jnp.exp(jnp.sum(lq2.astype(jnp.float32) * lk2.astype(jnp.float32)))
           + lambda_init)
    scale = B_HEAD_DIM ** -0.5

    def block(qb):
        s = jnp.einsum('bhcqd,bhcsd->bhcqs', qb, k).astype(jnp.float32) * scale
        p = jax.nn.softmax(s, axis=-1)
        a = (p[:, :, 0] - lam * p[:, :, 1]).astype(v.dtype)
        return jnp.einsum('bhqs,bhsv->bhqv', a, v)

    o = sweep_query_blocks(block, q)
    o = rms_norm(o, subln_g) * (1.0 - lambda_init)
    return o.transpose(0, 2, 1, 3).reshape(B, S, D_B)


def memory_cross_attention(x, mem, w_q, w_kv, w_o):
    B, S, D = x.shape
    M = mem.shape[1]
    q = (x @ w_q).reshape(B, S, MEM_HEADS, MEM_HEAD_DIM)
    k, v = jnp.split(mem @ w_kv, 2, axis=-1)
    k = k.reshape(B, M, MEM_HEADS, MEM_HEAD_DIM)
    v = v.reshape(B, M, MEM_HEADS, MEM_HEAD_DIM)
    s = jnp.einsum('bqhd,bmhd->bhqm', q, k).astype(jnp.float32) * (MEM_HEAD_DIM ** -0.5)
    p = jax.nn.softmax(s, axis=-1).astype(v.dtype)
    o = jnp.einsum('bhqm,bmhd->bqhd', p, v).reshape(B, S, D)
    return o @ w_o


def clamped_swiglu(g, u):
    g = jnp.minimum(g, SWIGLU_LIMIT)
    u = jnp.clip(u, -SWIGLU_LIMIT, SWIGLU_LIMIT)
    return g * jax.nn.sigmoid(SWIGLU_ALPHA * g) * (u + 1.0)


def routed_experts(x, w_r, b_r, w_g, b_g, w_u, b_u, w_d, b_d):
    B, S, D = x.shape
    xt = x.reshape(B * S, D)
    logits = (xt @ w_r + b_r).astype(jnp.float32)
    top_v, top_i = lax.top_k(logits, TOP_K)
    gates = jax.nn.softmax(top_v, axis=-1)
    combine = jnp.einsum('nk,nke->ne', gates,
                         jax.nn.one_hot(top_i, N_EXPERTS, dtype=jnp.float32)).astype(x.dtype)
    y = jnp.zeros_like(xt)
    for e in range(N_EXPERTS):
        h = clamped_swiglu(xt @ w_g[e] + b_g[e], xt @ w_u[e] + b_u[e])
        y = y + combine[:, e:e + 1] * (h @ w_d[e] + b_d[e])
    return y.reshape(B, S, D)


def setup_inputs(seed: int = 0) -> dict:
    key = jax.random.key(seed)
    ks = jax.random.split(key, 32)
    L, D, E, F = DEPTH, D_MODEL, N_EXPERTS, D_FF

    def nrm(k, shape, std):
        return std * jax.random.normal(k, shape, jnp.float32)

    def gain(k, shape):
        return 1.0 + nrm(k, shape, 0.02)

    return {
        'x': nrm(ks[0], (BATCH, SEQ, D), 1.0),
        'mem': nrm(ks[1], (BATCH, MEM_TOKENS, D), 1.0),
        'w_in': nrm(ks[2], (L, D, D_IN_PROJ), D ** -0.5),
        'a_q_norm': gain(ks[3], (L, A_HEAD_DIM)),
        'a_k_norm': gain(ks[4], (L, A_HEAD_DIM)),
        'b_lambda_q1': nrm(ks[5], (L, B_HEAD_DIM), 0.1),
        'b_lambda_k1': nrm(ks[6], (L, B_HEAD_DIM), 0.1),
        'b_lambda_q2': nrm(ks[7], (L, B_HEAD_DIM), 0.1),
        'b_lambda_k2': nrm(ks[8], (L, B_HEAD_DIM), 0.1),
        'b_subln': gain(ks[9], (L, B_VDIM)),
        'w_mix_out': nrm(ks[10], (L, D_MIX, D), D_MIX ** -0.5 * DEEPNORM_BETA),
        'ln1_g': gain(ks[11], (L, D)),
        'ln1_b': nrm(ks[12], (L, D), 0.02),
        'w_mem_q': nrm(ks[13], (L, D, D), D ** -0.5),
        'w_mem_kv': nrm(ks[14], (L, D, 2 * D), D ** -0.5),
        'w_mem_out': nrm(ks[15], (L, D, D), D ** -0.5 * DEEPNORM_BETA),
        'ln2_g': gain(ks[16], (L, D)),
        'ln2_b': nrm(ks[17], (L, D), 0.02),
        'w_router': nrm(ks[18], (L, D, E), D ** -0.5),
        'b_router': nrm(ks[19], (L, E), 0.01),
        'w_e_gate': nrm(ks[20], (L, E, D, F), D ** -0.5),
        'b_e_gate': nrm(ks[21], (L, E, F), 0.02),
        'w_e_up': nrm(ks[22], (L, E, D, F), D ** -0.5),
        'b_e_up': nrm(ks[23], (L, E, F), 0.02),
        'w_e_down': nrm(ks[24], (L, E, F, D), F ** -0.5 * DEEPNORM_BETA),
        'b_e_down': nrm(ks[25], (L, E, D), 0.02),
        'ln3_g': gain(ks[26], (L, D)),
        'ln3_b': nrm(ks[27], (L, D), 0.02),
    }


def reference(x, mem, w_in, a_q_norm, a_k_norm, b_lambda_q1, b_lambda_k1, b_lambda_q2,
              b_lambda_k2, b_subln, w_mix_out, ln1_g, ln1_b, w_mem_q, w_mem_kv, w_mem_out,
              ln2_g, ln2_b, w_router, b_router, w_e_gate, b_e_gate, w_e_up, b_e_up,
              w_e_down, b_e_down, ln3_g, ln3_b):
    B, S, D = x.shape
    ROWS = S // GRID_W
    grid = jnp.stack(jnp.meshgrid(jnp.arange(ROWS, dtype=jnp.int32),
                                  jnp.arange(GRID_W, dtype=jnp.int32), indexing='ij'), axis=-1)
    grid = grid.reshape(S, 2)
    cs_row = rope_cos_sin(grid[:, 0], A_HEAD_DIM // 2)
    cs_col = rope_cos_sin(grid[:, 1], A_HEAD_DIM // 2)
    cs_seq = rope_cos_sin(jnp.arange(S, dtype=jnp.int32), B_HEAD_DIM)

    for layer in range(DEPTH):
        lambda_init = 0.8 - 0.6 * math.exp(-0.3 * layer)
        h = x @ w_in[layer]
        qa, ka, va, qb, kb, vb = jnp.split(h, IN_SPLITS, axis=-1)
        out_a = gqa_axial_attention(qa.reshape(B, S, A_HEADS, A_HEAD_DIM),
                                    ka.reshape(B, S, A_KV_HEADS, A_HEAD_DIM),
                                    va.reshape(B, S, A_KV_HEADS, A_HEAD_DIM),
                                    a_q_norm[layer], a_k_norm[layer], cs_row, cs_col)
        out_b = differential_attention(qb.reshape(B, S, B_HEADS, 2, B_HEAD_DIM),
                                       kb.reshape(B, S, B_HEADS, 2, B_HEAD_DIM),
                                       vb.reshape(B, S, B_HEADS, B_VDIM),
                                       b_lambda_q1[layer], b_lambda_k1[layer],
                                       b_lambda_q2[layer], b_lambda_k2[layer],
                                       b_subln[layer], cs_seq, lambda_init)
        mix = jnp.concatenate([out_a, out_b], axis=-1) @ w_mix_out[layer]
        x = layer_norm(DEEPNORM_ALPHA * x + mix, ln1_g[layer], ln1_b[layer])
        xa = memory_cross_attention(x, mem, w_mem_q[layer], w_mem_kv[layer], w_mem_out[layer])
        x = layer_norm(DEEPNORM_ALPHA * x + xa, ln2_g[layer], ln2_b[layer])
        ff = routed_experts(x, w_router[layer], b_router[layer], w_e_gate[layer], b_e_gate[layer],
                            w_e_up[layer], b_e_up[layer], w_e_down[layer], b_e_down[layer])
        x = layer_norm(DEEPNORM_ALPHA * x + ff, ln3_g[layer], ln3_b[layer])
    return x
```

```python
import functools
import math

import jax
import jax.numpy as jnp
from jax import lax
from jax.experimental import pallas as pl
from jax.experimental.pallas import tpu as pltpu

F32 = jnp.float32
BF16 = jnp.bfloat16

GRID_W = 64
ROPE_THETA = 10000.0
A_HEADS = 8
A_KV_HEADS = 2
A_GROUP = A_HEADS // A_KV_HEADS
B_HEADS = 4
MEM_HEADS = 4
TOP_K = 4
SWIGLU_LIMIT = 7.0
SWIGLU_ALPHA = 1.702
LN_EPS = 1e-5
RMS_EPS = 1e-6

VMEM_LIMIT_BYTES = 56 * 1024 * 1024
LANES = 128


def _cparams(*sem):
    return pltpu.CompilerParams(dimension_semantics=sem, vmem_limit_bytes=VMEM_LIMIT_BYTES)


def _layer_norm(x, g, b):
    mu = jnp.mean(x, axis=-1, keepdims=True)
    xc = x - mu
    var = jnp.mean(xc * xc, axis=-1, keepdims=True)
    return xc * lax.rsqrt(var + LN_EPS) * g + b


def _rms_norm(x, g):
    return x * lax.rsqrt(jnp.mean(x * x, axis=-1, keepdims=True) + RMS_EPS) * g


def _swap_halves(x, seg):
    h = seg // 2
    parts = []
    for s in range(0, x.shape[-1], seg):
        parts += [x[:, s + h:s + seg], x[:, s:s + h]]
    return jnp.concatenate(parts, axis=-1)


def _rope(x, cos, sin_signed, seg):
    return x * cos + _swap_halves(x, seg) * sin_signed


def _softmax_parts(s):
    m = jnp.max(s, axis=-1, keepdims=True)
    e = jnp.exp(s - m)
    return e, jnp.sum(e, axis=-1, keepdims=True)


def _in_proj_kernel(x_ref, w_ref, *out_refs, splits):
    xb = x_ref[...].astype(BF16)
    for o_ref, (c0, c1) in zip(out_refs, splits):
        o_ref[...] = jnp.dot(xb, w_ref[:, c0:c1], preferred_element_type=F32).astype(o_ref.dtype)


def _in_proj(x2d, w_bf, splits, tm):
    n, d = x2d.shape
    widths = [c1 - c0 for c0, c1 in splits]
    return pl.pallas_call(
        functools.partial(_in_proj_kernel, splits=splits),
        grid=(n // tm,),
        in_specs=[pl.BlockSpec((tm, d), lambda i: (i, 0)),
                  pl.BlockSpec(w_bf.shape, lambda i: (0, 0))],
        out_specs=[pl.BlockSpec((tm, w), lambda i: (i, 0)) for w in widths],
        out_shape=[jax.ShapeDtypeStruct((n, w), BF16) for w in widths],
        compiler_params=_cparams("parallel"),
        name="in_proj",
    )(x2d, w_bf)


def _attn_a_kernel(q_ref, k_ref, v_ref, cq_ref, sq_ref, ck_ref, sk_ref, gq_ref, gk_ref,
                   o_ref, kn_ref, *, hd):
    @pl.when(pl.program_id(1) == 0)
    def _():
        for kv in range(A_KV_HEADS):
            k = _rms_norm(k_ref[:, kv * hd:(kv + 1) * hd].astype(F32), gk_ref[...])
            kn_ref[kv] = _rope(k, ck_ref[...], sk_ref[...], hd // 2).astype(BF16)

    scale = hd ** -0.5
    outs = []
    for h in range(A_HEADS):
        kv = h // A_GROUP
        q = _rms_norm(q_ref[:, h * hd:(h + 1) * hd].astype(F32), gq_ref[...])
        q = (_rope(q, cq_ref[...], sq_ref[...], hd // 2) * scale).astype(BF16)
        s = lax.dot_general(q, kn_ref[kv], (((1,), (1,)), ((), ())), preferred_element_type=F32)
        e, l = _softmax_parts(s)
        o = jnp.dot(e.astype(BF16), v_ref[:, kv * hd:(kv + 1) * hd], preferred_element_type=F32)
        outs.append(o * (1.0 / l))
    o_ref[...] = jnp.concatenate(outs, axis=-1).astype(o_ref.dtype)


def _attn_a(qa, ka, va, cos, sin, gq, gk, batch, seq, tq):
    n, dq = qa.shape
    hd = dq // A_HEADS
    nq = seq // tq
    row = lambda b, i: (b * nq + i, 0)
    per_b = lambda b, i: (b, 0)
    const = lambda b, i: (0, 0)
    return pl.pallas_call(
        functools.partial(_attn_a_kernel, hd=hd),
        grid=(batch, nq),
        in_specs=[pl.BlockSpec((tq, dq), row),
                  pl.BlockSpec((seq, ka.shape[1]), per_b),
                  pl.BlockSpec((seq, va.shape[1]), per_b),
                  pl.BlockSpec((tq, hd), lambda b, i: (i, 0)),
                  pl.BlockSpec((tq, hd), lambda b, i: (i, 0)),
                  pl.BlockSpec((seq, hd), const),
                  pl.BlockSpec((seq, hd), const),
                  pl.BlockSpec((1, hd), const),
                  pl.BlockSpec((1, hd), const)],
        out_specs=pl.BlockSpec((tq, dq), row),
        out_shape=jax.ShapeDtypeStruct((n, dq), BF16),
        scratch_shapes=[pltpu.VMEM((A_KV_HEADS, seq, hd), BF16)],
        compiler_params=_cparams("arbitrary", "arbitrary"),
        name="attn_a",
    )(qa, ka, va, cos, sin, cos, sin, gq, gk)


def _attn_b_kernel(q_ref, k_ref, v_ref, cq_ref, sq_ref, ck_ref, sk_ref,
                   lq1_ref, lk1_ref, lq2_ref, lk2_ref, g_ref, o_ref, kr_ref, *, hd, lambda_init):
    nmaps = 2 * B_HEADS

    @pl.when(pl.program_id(1) == 0)
    def _():
        for c in range(nmaps):
            k = k_ref[:, c * hd:(c + 1) * hd].astype(F32)
            kr_ref[c] = _rope(k, ck_ref[...], sk_ref[...], hd).astype(BF16)

    lam = (jnp.exp(jnp.sum(lq1_ref[...] * lk1_ref[...], axis=-1, keepdims=True))
           - jnp.exp(jnp.sum(lq2_ref[...] * lk2_ref[...], axis=-1, keepdims=True))
           + lambda_init)
    scale = hd ** -0.5
    vd = 2 * hd
    outs = []
    for h in range(B_HEADS):
        es, ls = [], []
        for c in range(2):
            m = 2 * h + c
            q = q_ref[:, m * hd:(m + 1) * hd].astype(F32)
            q = (_rope(q, cq_ref[...], sq_ref[...], hd) * scale).astype(BF16)
            s = lax.dot_general(q, kr_ref[m], (((1,), (1,)), ((), ())), preferred_element_type=F32)
            e, l = _softmax_parts(s)
            es.append(e)
            ls.append(l)
        a = es[0] * (1.0 / ls[0]) - es[1] * (lam / ls[1])
        o = jnp.dot(a.astype(BF16), v_ref[:, h * vd:(h + 1) * vd], preferred_element_type=F32)
        outs.append(_rms_norm(o, g_ref[...]) * (1.0 - lambda_init))
    o_ref[...] = jnp.concatenate(outs, axis=-1).astype(o_ref.dtype)


def _attn_b(qb, kb, vb, cos, sin, lq1, lk1, lq2, lk2, subln, batch, seq, tq, lambda_init):
    n, dq = qb.shape
    hd = dq // (2 * B_HEADS)
    nq = seq // tq
    row = lambda b, i: (b * nq + i, 0)
    per_b = lambda b, i: (b, 0)
    const = lambda b, i: (0, 0)
    vec = pl.BlockSpec((1, hd), const)
    return pl.pallas_call(
        functools.partial(_attn_b_kernel, hd=hd, lambda_init=lambda_init),
        grid=(batch, nq),
        in_specs=[pl.BlockSpec((tq, dq), row),
                  pl.BlockSpec((seq, dq), per_b),
                  pl.BlockSpec((seq, vb.shape[1]), per_b),
                  pl.BlockSpec((tq, hd), lambda b, i: (i, 0)),
                  pl.BlockSpec((tq, hd), lambda b, i: (i, 0)),
                  pl.BlockSpec((seq, hd), const),
                  pl.BlockSpec((seq, hd), const),
                  vec, vec, vec, vec,
                  pl.BlockSpec((1, 2 * hd), const)],
        out_specs=pl.BlockSpec((tq, vb.shape[1]), row),
        out_shape=jax.ShapeDtypeStruct((n, vb.shape[1]), BF16),
        scratch_shapes=[pltpu.VMEM((2 * B_HEADS, seq, hd), BF16)],
        compiler_params=_cparams("arbitrary", "arbitrary"),
        name="attn_b",
    )(qb, kb, vb, cos, sin, cos, sin, lq1, lk1, lq2, lk2, subln)


def _matmul_kernel(x_ref, w_ref, o_ref):
    o_ref[...] = jnp.dot(x_ref[...].astype(BF16), w_ref[...],
                         preferred_element_type=F32).astype(o_ref.dtype)


def _mem_kv(mem2d, w_bf, tm):
    n, d = mem2d.shape
    return pl.pallas_call(
        _matmul_kernel,
        grid=(n // tm,),
        in_specs=[pl.BlockSpec((tm, d), lambda i: (i, 0)),
                  pl.BlockSpec(w_bf.shape, lambda i: (0, 0))],
        out_specs=pl.BlockSpec((tm, w_bf.shape[1]), lambda i: (i, 0)),
        out_shape=jax.ShapeDtypeStruct((n, w_bf.shape[1]), BF16),
        compiler_params=_cparams("parallel"),
        name="mem_kv",
    )(mem2d, w_bf)


def _mix_xattn_kernel(oa_ref, ob_ref, x_ref, wmix_ref, g1_ref, b1_ref, wq_ref, kv_ref, wo_ref,
                      g2_ref, b2_ref, wr_ref, br_ref,
                      x2_ref, idx_ref, gate_ref, *, alpha, n_experts):
    da = oa_ref.shape[1]
    mix = (jnp.dot(oa_ref[...], wmix_ref[:da, :], preferred_element_type=F32)
           + jnp.dot(ob_ref[...], wmix_ref[da:, :], preferred_element_type=F32))
    x1 = _layer_norm(alpha * x_ref[...] + mix, g1_ref[...], b1_ref[...])

    d = x1.shape[1]
    hd = d // MEM_HEADS
    q = jnp.dot(x1.astype(BF16), wq_ref[...], preferred_element_type=F32).astype(BF16)
    scale = hd ** -0.5
    outs = []
    for h in range(MEM_HEADS):
        s = lax.dot_general(q[:, h * hd:(h + 1) * hd], kv_ref[:, h * hd:(h + 1) * hd],
                            (((1,), (1,)), ((), ())), preferred_element_type=F32) * scale
        e, l = _softmax_parts(s)
        o = jnp.dot(e.astype(BF16), kv_ref[:, d + h * hd:d + (h + 1) * hd],
                    preferred_element_type=F32)
        outs.append((o * (1.0 / l)).astype(BF16))
    xa = jnp.dot(jnp.concatenate(outs, axis=-1), wo_ref[...], preferred_element_type=F32)
    x2 = _layer_norm(alpha * x1 + xa, g2_ref[...], b2_ref[...])
    x2_ref[...] = x2
    x2b = x2.astype(BF16)

    logits = jnp.dot(x2b, wr_ref[...], preferred_element_type=F32) + br_ref[...]
    col = lax.broadcasted_iota(jnp.int32, logits.shape, 1).astype(F32)
    lane = lax.broadcasted_iota(jnp.int32, idx_ref.shape, 1)
    vals, idxs = [], []
    for _ in range(TOP_K):
        m = jnp.max(logits, axis=-1, keepdims=True)
        idx = jnp.min(jnp.where(logits == m, col, float(n_experts)), axis=-1, keepdims=True)
        vals.append(m)
        idxs.append(idx)
        logits = jnp.where(col == idx, -jnp.inf, logits)
    es = [jnp.exp(v - vals[0]) for v in vals]
    den = es[0] + es[1] + es[2] + es[3]
    idx_out = jnp.zeros(idx_ref.shape, F32)
    gate_out = jnp.zeros(gate_ref.shape, F32)
    for k in range(TOP_K):
        idx_out = jnp.where(lane == k, idxs[k], idx_out)
        gate_out = jnp.where(lane == k, es[k] / den, gate_out)
    idx_ref[...] = idx_out.astype(jnp.int32)
    gate_ref[...] = gate_out


def _mix_xattn(oa, ob, x2d, wmix, g1, b1, wq, kvmem, wo, g2, b2, wr, br, seq, mem_tokens, tm, alpha):
    n, d = x2d.shape
    tiles_per_b = seq // tm
    row = lambda i: (i, 0)
    const = lambda i: (0, 0)
    full = lambda a: pl.BlockSpec(a.shape, const)
    return pl.pallas_call(
        functools.partial(_mix_xattn_kernel, alpha=alpha, n_experts=wr.shape[1]),
        grid=(n // tm,),
        in_specs=[pl.BlockSpec((tm, oa.shape[1]), row),
                  pl.BlockSpec((tm, ob.shape[1]), row),
                  pl.BlockSpec((tm, d), row),
                  full(wmix), full(g1), full(b1), full(wq),
                  pl.BlockSpec((mem_tokens, kvmem.shape[1]), lambda i: (i // tiles_per_b, 0)),
                  full(wo), full(g2), full(b2), full(wr), full(br)],
        out_specs=[pl.BlockSpec((tm, d), row),
                   pl.BlockSpec((tm, LANES), row), pl.BlockSpec((tm, LANES), row)],
        out_shape=[jax.ShapeDtypeStruct((n, d), F32),
                   jax.ShapeDtypeStruct((n, LANES), jnp.int32),
                   jax.ShapeDtypeStruct((n, LANES), F32)],
        compiler_params=_cparams("parallel"),
        name="mix_xattn",
    )(oa, ob, x2d, wmix, g1, b1, wq, kvmem, wo, g2, b2, wr, br)


def _route_meta(top_i, n_experts, tg, n_tiles):
    n, k = top_i.shape
    flat = top_i.reshape(-1)
    onehot = (flat[:, None] == jnp.arange(n_experts, dtype=jnp.int32)[None, :]).astype(jnp.int32)
    csum = jnp.cumsum(onehot, axis=0)
    counts = csum[-1]
    rank = jnp.sum((csum - 1) * onehot, axis=1)
    padded = ((counts + tg - 1) // tg) * tg
    pad_end = jnp.cumsum(padded)
    pad_start = pad_end - padded
    pair_slot = pad_start[flat] + rank
    slot_token = jnp.zeros((n_tiles * tg,), jnp.int32).at[pair_slot].set(
        jnp.arange(n * k, dtype=jnp.int32) // k)
    tile_start = jnp.arange(n_tiles, dtype=jnp.int32) * tg
    tile_expert = jnp.minimum(jnp.searchsorted(pad_end, tile_start, side="right"),
                              n_experts - 1).astype(jnp.int32)
    n_used = (pad_end[-1] // tg).astype(jnp.int32).reshape(1)
    return pair_slot.astype(jnp.int32), slot_token, tile_expert, n_used


def _gather_kernel(nused_ref, tok_ref, x_hbm, o_hbm, sem, *, rows):
    i = pl.program_id(0)

    def copy(r):
        return pltpu.make_async_copy(x_hbm.at[pl.ds(tok_ref[0, 0, r], 1)],
                                     o_hbm.at[pl.ds(i * rows + r, 1)], sem)

    @pl.when(i < nused_ref[0])
    def _():
        def start(r, c):
            copy(r).start()
            return c
        lax.fori_loop(0, rows, start, 0, unroll=8)

        def wait(r, c):
            copy(r).wait()
            return c
        lax.fori_loop(0, rows, wait, 0, unroll=8)


def _gather_rows(x2b, slot_token, n_used, n_tiles, tg):
    n, d = x2b.shape
    grid_spec = pltpu.PrefetchScalarGridSpec(
        num_scalar_prefetch=1,
        grid=(n_tiles,),
        in_specs=[pl.BlockSpec((1, 1, tg), lambda i, nu: (i, 0, 0), memory_space=pltpu.SMEM),
                  pl.BlockSpec(memory_space=pl.ANY)],
        out_specs=pl.BlockSpec(memory_space=pl.ANY),
        scratch_shapes=[pltpu.SemaphoreType.DMA(())],
    )
    return pl.pallas_call(
        functools.partial(_gather_kernel, rows=tg),
        grid_spec=grid_spec,
        out_shape=jax.ShapeDtypeStruct((n_tiles * tg, d), x2b.dtype),
        compiler_params=_cparams("arbitrary"),
        name="gather_rows",
    )(n_used, slot_token.reshape(n_tiles, 1, tg), x2b)


def _experts_kernel(te_ref, nused_ref, x_ref, wg_ref, bg_ref, wu_ref, bu_ref, wd_ref, bd_ref,
                    o_ref, wgb_ref, wub_ref, wdb_ref):
    i = pl.program_id(0)
    prev = te_ref[jnp.maximum(i - 1, 0)]
    used = i < nused_ref[0]

    @pl.when(used & ((i == 0) | (te_ref[i] != prev)))
    def _():
        wgb_ref[...] = wg_ref[0].astype(BF16)
        wub_ref[...] = wu_ref[0].astype(BF16)
        wdb_ref[...] = wd_ref[0].astype(BF16)

    @pl.when(used)
    def _():
        x = x_ref[...].astype(BF16)
        g = jnp.dot(x, wgb_ref[...], preferred_element_type=F32) + bg_ref[0]
        u = jnp.dot(x, wub_ref[...], preferred_element_type=F32) + bu_ref[0]
        g = jnp.minimum(g, SWIGLU_LIMIT)
        u = jnp.clip(u, -SWIGLU_LIMIT, SWIGLU_LIMIT)
        h = g * (1.0 / (1.0 + jnp.exp(-SWIGLU_ALPHA * g))) * (u + 1.0)
        y = jnp.dot(h.astype(BF16), wdb_ref[...], preferred_element_type=F32) + bd_ref[0]
        o_ref[...] = y.astype(o_ref.dtype)

    @pl.when(jnp.logical_not(used))
    def _():
        o_ref[...] = jnp.zeros(o_ref.shape, o_ref.dtype)


def _experts(xs, tile_expert, n_used, wg, bg, wu, bu, wd, bd, tg):
    p, d = xs.shape
    e, _, f = wg.shape
    n_tiles = p // tg
    xrow = lambda i, te, nu: (jnp.minimum(i, nu[0] - 1), 0)
    wmap = lambda i, te, nu: (te[i], 0, 0)
    grid_spec = pltpu.PrefetchScalarGridSpec(
        num_scalar_prefetch=2,
        grid=(n_tiles,),
        in_specs=[pl.BlockSpec((tg, d), xrow),
                  pl.BlockSpec((1, d, f), wmap), pl.BlockSpec((1, 1, f), wmap),
                  pl.BlockSpec((1, d, f), wmap), pl.BlockSpec((1, 1, f), wmap),
                  pl.BlockSpec((1, f, d), wmap), pl.BlockSpec((1, 1, d), wmap)],
        out_specs=pl.BlockSpec((tg, d), lambda i, te, nu: (i, 0)),
        scratch_shapes=[pltpu.VMEM((d, f), BF16), pltpu.VMEM((d, f), BF16), pltpu.VMEM((f, d), BF16)],
    )
    return pl.pallas_call(
        _experts_kernel,
        grid_spec=grid_spec,
        out_shape=jax.ShapeDtypeStruct((p, d), F32),
        compiler_params=_cparams("arbitrary"),
        name="experts",
    )(tile_expert, n_used, xs, wg, bg.reshape(e, 1, f), wu, bu.reshape(e, 1, f),
      wd, bd.reshape(e, 1, d))


def _combine_kernel(slot_ref, ys_hbm, x2_ref, gate_ref, g_ref, b_ref, o_ref, buf_ref, sem, *, alpha):
    tc = x2_ref.shape[0]

    def copy(t, k):
        return pltpu.make_async_copy(ys_hbm.at[pl.ds(slot_ref[0, 0, t * TOP_K + k], 1)],
                                     buf_ref.at[k, pl.ds(t, 1)], sem)

    def start(t, c):
        for k in range(TOP_K):
            copy(t, k).start()
        return c
    lax.fori_loop(0, tc, start, 0, unroll=4)

    def wait(t, c):
        for k in range(TOP_K):
            copy(t, k).wait()
        return c
    lax.fori_loop(0, tc, wait, 0, unroll=4)

    gates = gate_ref[...]
    y = jnp.zeros(x2_ref.shape, F32)
    for k in range(TOP_K):
        y = y + gates[:, k:k + 1] * buf_ref[k].astype(F32)
    o_ref[...] = _layer_norm(alpha * x2_ref[...] + y, g_ref[...], b_ref[...])


def _combine(ys, pair_slot, x2, gates, g3, b3, tc, alpha):
    n, d = x2.shape
    nt = n // tc
    row = lambda i: (i, 0)
    const = lambda i: (0, 0)
    return pl.pallas_call(
        functools.partial(_combine_kernel, alpha=alpha),
        grid=(nt,),
        in_specs=[pl.BlockSpec((1, 1, tc * TOP_K), lambda i: (i, 0, 0), memory_space=pltpu.SMEM),
                  pl.BlockSpec(memory_space=pl.ANY),
                  pl.BlockSpec((tc, d), row),
                  pl.BlockSpec((tc, LANES), row),
                  pl.BlockSpec(g3.shape, const), pl.BlockSpec(b3.shape, const)],
        out_specs=pl.BlockSpec((tc, d), row),
        out_shape=jax.ShapeDtypeStruct((n, d), F32),
        scratch_shapes=[pltpu.VMEM((TOP_K, tc, d), ys.dtype), pltpu.SemaphoreType.DMA(())],
        compiler_params=_cparams("arbitrary"),
        name="combine",
    )(pair_slot.reshape(nt, 1, tc * TOP_K), ys, x2, gates, g3, b3)


def _rope_tables(seq, a_hd, b_hd):
    def cos_sin(pos, dim):
        inv = ROPE_THETA ** (-jnp.arange(0, dim, 2, dtype=F32) / dim)
        ang = pos.astype(F32)[:, None] * inv[None, :]
        return jnp.cos(ang), jnp.sin(ang)

    t = jnp.arange(seq, dtype=jnp.int32)
    cr, sr = cos_sin(t // GRID_W, a_hd // 2)
    cc, sc = cos_sin(t % GRID_W, a_hd // 2)
    cs, ss = cos_sin(t, b_hd)
    cos_a = jnp.concatenate([cr, cr, cc, cc], axis=-1)
    sin_a = jnp.concatenate([-sr, sr, -sc, sc], axis=-1)
    cos_b = jnp.concatenate([cs, cs], axis=-1)
    sin_b = jnp.concatenate([-ss, ss], axis=-1)
    return cos_a, sin_a, cos_b, sin_b


def _pick_tile(n, pref):
    t = min(pref, n)
    while n % t:
        t //= 2
    return t


def kernel(x, mem, w_in, a_q_norm, a_k_norm, b_lambda_q1, b_lambda_k1, b_lambda_q2, b_lambda_k2, b_subln, w_mix_out, ln1_g, ln1_b, w_mem_q, w_mem_kv, w_mem_out, ln2_g, ln2_b, w_router, b_router, w_e_gate, b_e_gate, w_e_up, b_e_up, w_e_down, b_e_down, ln3_g, ln3_b):
    batch, seq, d = x.shape
    depth = w_in.shape[0]
    n = batch * seq
    mem_tokens = mem.shape[1]
    n_experts = w_router.shape[-1]
    a_hd = a_q_norm.shape[-1]
    b_hd = b_lambda_q1.shape[-1]
    alpha = (2.0 * depth) ** 0.25

    a_q, a_kv = A_HEADS * a_hd, A_KV_HEADS * a_hd
    b_qk, b_v = B_HEADS * 2 * b_hd, B_HEADS * 2 * b_hd
    edges = [0, a_q, a_q + a_kv, a_q + 2 * a_kv, a_q + 2 * a_kv + b_qk,
             a_q + 2 * a_kv + 2 * b_qk, a_q + 2 * a_kv + 2 * b_qk + b_v]
    splits = tuple(zip(edges[:-1], edges[1:]))
    assert edges[-1] == w_in.shape[-1]

    tm = _pick_tile(seq, 512)
    tq = _pick_tile(seq, 256)
    tg = 512
    tc = _pick_tile(n, 256)
    n_tiles = (n * TOP_K) // tg + n_experts
    cos_a, sin_a, cos_b, sin_b = _rope_tables(seq, a_hd, b_hd)
    row2 = lambda v: v.reshape(1, -1)

    h = x.reshape(n, d)
    mem2d = mem.reshape(batch * mem_tokens, d)
    for layer in range(depth):
        lambda_init = 0.8 - 0.6 * math.exp(-0.3 * layer)
        qa, ka, va, qb, kb, vb = _in_proj(h, w_in[layer].astype(BF16), splits, tm)
        oa = _attn_a(qa, ka, va, cos_a, sin_a, row2(a_q_norm[layer]), row2(a_k_norm[layer]),
                     batch, seq, tq)
        ob = _attn_b(qb, kb, vb, cos_b, sin_b, row2(b_lambda_q1[layer]), row2(b_lambda_k1[layer]),
                     row2(b_lambda_q2[layer]), row2(b_lambda_k2[layer]), row2(b_subln[layer]),
                     batch, seq, tq, lambda_init)
        kvmem = _mem_kv(mem2d, w_mem_kv[layer].astype(BF16), _pick_tile(batch * mem_tokens, 512))
        x2, idx, gates = _mix_xattn(
            oa, ob, h, w_mix_out[layer].astype(BF16), row2(ln1_g[layer]), row2(ln1_b[layer]),
            w_mem_q[layer].astype(BF16), kvmem, w_mem_out[layer].astype(BF16),
            row2(ln2_g[layer]), row2(ln2_b[layer]), w_router[layer].astype(BF16),
            row2(b_router[layer]), seq, mem_tokens, tm, alpha)
        pair_slot, slot_token, tile_expert, n_used = _route_meta(idx[:, :TOP_K], n_experts, tg, n_tiles)
        xs = _gather_rows(x2, slot_token, n_used, n_tiles, tg)
        ys = _experts(xs, tile_expert, n_used, w_e_gate[layer], b_e_gate[layer], w_e_up[layer],
                      b_e_up[layer], w_e_down[layer], b_e_down[layer], tg)
        h = _combine(ys, pair_slot, x2, gates, row2(ln3_g[layer]), row2(ln3_b[layer]), tc, alpha)
    return h.reshape(batch, seq, d)
```

```python
import functools
import math

import jax
import jax.numpy as jnp
from jax import lax
from jax.experimental import pallas as pl
from jax.experimental.pallas import tpu as pltpu

F32 = jnp.float32
BF16 = jnp.bfloat16

GRID_W = 64
ROPE_THETA = 10000.0
A_HEADS = 8
A_KV_HEADS = 2
A_GROUP = A_HEADS // A_KV_HEADS
B_HEADS = 4
MEM_HEADS = 4
TOP_K = 4
SWIGLU_LIMIT = 7.0
SWIGLU_ALPHA = 1.702
LN_EPS = 1e-5
RMS_EPS = 1e-6

VMEM_LIMIT_BYTES = 56 * 1024 * 1024
LANES = 128
SUBLANES = 8
CODE_EXPERT_SHIFT = 20
CODE_EXPERT_STRIDE = 1 << CODE_EXPERT_SHIFT


def _cparams(*sem):
    return pltpu.CompilerParams(dimension_semantics=sem, vmem_limit_bytes=VMEM_LIMIT_BYTES)


def _layer_norm(x, g, b):
    mu = jnp.mean(x, axis=-1, keepdims=True)
    xc = x - mu
    var = jnp.mean(xc * xc, axis=-1, keepdims=True)
    return xc * lax.rsqrt(var + LN_EPS) * g + b


def _rms_norm(x, g):
    return x * lax.rsqrt(jnp.mean(x * x, axis=-1, keepdims=True) + RMS_EPS) * g


def _swap_halves(x, seg):
    h = seg // 2
    parts = []
    for s in range(0, x.shape[-1], seg):
        parts += [x[:, s + h:s + seg], x[:, s:s + h]]
    return jnp.concatenate(parts, axis=-1)


def _rope(x, cos, sin_signed, seg):
    return x * cos + _swap_halves(x, seg) * sin_signed


def _softmax_parts(s):
    m = jnp.max(s, axis=-1, keepdims=True)
    e = jnp.exp(s - m)
    return e, jnp.sum(e, axis=-1, keepdims=True)


def _in_proj_kernel(x_ref, w_ref, *out_refs, splits):
    xb = x_ref[...].astype(BF16)
    for o_ref, (c0, c1) in zip(out_refs, splits):
        o_ref[...] = jnp.dot(xb, w_ref[:, c0:c1], preferred_element_type=F32).astype(o_ref.dtype)


def _in_proj(x2d, w_bf, splits, tm):
    n, d = x2d.shape
    widths = [c1 - c0 for c0, c1 in splits]
    return pl.pallas_call(
        functools.partial(_in_proj_kernel, splits=splits),
        grid=(n // tm,),
        in_specs=[pl.BlockSpec((tm, d), lambda i: (i, 0)),
                  pl.BlockSpec(w_bf.shape, lambda i: (0, 0))],
        out_specs=[pl.BlockSpec((tm, w), lambda i: (i, 0)) for w in widths],
        out_shape=[jax.ShapeDtypeStruct((n, w), BF16) for w in widths],
        compiler_params=_cparams("parallel"),
        name="in_proj",
    )(x2d, w_bf)


def _attn_a_kernel(q_ref, k_ref, v_ref, cq_ref, sq_ref, ck_ref, sk_ref, gq_ref, gk_ref,
                   o_ref, kn_ref, *, hd):
    @pl.when(pl.program_id(1) == 0)
    def _():
        for kv in range(A_KV_HEADS):
            k = _rms_norm(k_ref[:, kv * hd:(kv + 1) * hd].astype(F32), gk_ref[...])
            kn_ref[kv] = _rope(k, ck_ref[...], sk_ref[...], hd // 2).astype(BF16)

    scale = hd ** -0.5
    outs = []
    for h in range(A_HEADS):
        kv = h // A_GROUP
        q = _rms_norm(q_ref[:, h * hd:(h + 1) * hd].astype(F32), gq_ref[...])
        q = (_rope(q, cq_ref[...], sq_ref[...], hd // 2) * scale).astype(BF16)
        s = lax.dot_general(q, kn_ref[kv], (((1,), (1,)), ((), ())), preferred_element_type=F32)
        e, l = _softmax_parts(s)
        o = jnp.dot(e.astype(BF16), v_ref[:, kv * hd:(kv + 1) * hd], preferred_element_type=F32)
        outs.append(o * (1.0 / l))
    o_ref[...] = jnp.concatenate(outs, axis=-1).astype(o_ref.dtype)


def _attn_a(qa, ka, va, cos, sin, gq, gk, batch, seq, tq):
    n, dq = qa.shape
    hd = dq // A_HEADS
    nq = seq // tq
    row = lambda b, i: (b * nq + i, 0)
    per_b = lambda b, i: (b, 0)
    const = lambda b, i: (0, 0)
    return pl.pallas_call(
        functools.partial(_attn_a_kernel, hd=hd),
        grid=(batch, nq),
        in_specs=[pl.BlockSpec((tq, dq), row),
                  pl.BlockSpec((seq, ka.shape[1]), per_b),
                  pl.BlockSpec((seq, va.shape[1]), per_b),
                  pl.BlockSpec((tq, hd), lambda b, i: (i, 0)),
                  pl.BlockSpec((tq, hd), lambda b, i: (i, 0)),
                  pl.BlockSpec((seq, hd), const),
                  pl.BlockSpec((seq, hd), const),
                  pl.BlockSpec((1, hd), const),
                  pl.BlockSpec((1, hd), const)],
        out_specs=pl.BlockSpec((tq, dq), row),
        out_shape=jax.ShapeDtypeStruct((n, dq), BF16),
        scratch_shapes=[pltpu.VMEM((A_KV_HEADS, seq, hd), BF16)],
        compiler_params=_cparams("arbitrary", "arbitrary"),
        name="attn_a",
    )(qa, ka, va, cos, sin, cos, sin, gq, gk)


def _attn_b_kernel(q_ref, k_ref, v_ref, cq_ref, sq_ref, ck_ref, sk_ref,
                   lq1_ref, lk1_ref, lq2_ref, lk2_ref, g_ref, o_ref, kr_ref, *, hd, lambda_init):
    nmaps = 2 * B_HEADS

    @pl.when(pl.program_id(1) == 0)
    def _():
        for c in range(nmaps):
            k = k_ref[:, c * hd:(c + 1) * hd].astype(F32)
            kr_ref[c] = _rope(k, ck_ref[...], sk_ref[...], hd).astype(BF16)

    lam = (jnp.exp(jnp.sum(lq1_ref[...] * lk1_ref[...], axis=-1, keepdims=True))
           - jnp.exp(jnp.sum(lq2_ref[...] * lk2_ref[...], axis=-1, keepdims=True))
           + lambda_init)
    scale = hd ** -0.5
    vd = 2 * hd
    outs = []
    for h in range(B_HEADS):
        es, ls = [], []
        for c in range(2):
            m = 2 * h + c
            q = q_ref[:, m * hd:(m + 1) * hd].astype(F32)
            q = (_rope(q, cq_ref[...], sq_ref[...], hd) * scale).astype(BF16)
            s = lax.dot_general(q, kr_ref[m], (((1,), (1,)), ((), ())), preferred_element_type=F32)
            e, l = _softmax_parts(s)
            es.append(e)
            ls.append(l)
        a = es[0] * (1.0 / ls[0]) - es[1] * (lam / ls[1])
        o = jnp.dot(a.astype(BF16), v_ref[:, h * vd:(h + 1) * vd], preferred_element_type=F32)
        outs.append(_rms_norm(o, g_ref[...]) * (1.0 - lambda_init))
    o_ref[...] = jnp.concatenate(outs, axis=-1).astype(o_ref.dtype)


def _attn_b(qb, kb, vb, cos, sin, lq1, lk1, lq2, lk2, subln, batch, seq, tq, lambda_init):
    n, dq = qb.shape
    hd = dq // (2 * B_HEADS)
    nq = seq // tq
    row = lambda b, i: (b * nq + i, 0)
    per_b = lambda b, i: (b, 0)
    const = lambda b, i: (0, 0)
    vec = pl.BlockSpec((1, hd), const)
    return pl.pallas_call(
        functools.partial(_attn_b_kernel, hd=hd, lambda_init=lambda_init),
        grid=(batch, nq),
        in_specs=[pl.BlockSpec((tq, dq), row),
                  pl.BlockSpec((seq, dq), per_b),
                  pl.BlockSpec((seq, vb.shape[1]), per_b),
                  pl.BlockSpec((tq, hd), lambda b, i: (i, 0)),
                  pl.BlockSpec((tq, hd), lambda b, i: (i, 0)),
                  pl.BlockSpec((seq, hd), const),
                  pl.BlockSpec((seq, hd), const),
                  vec, vec, vec, vec,
                  pl.BlockSpec((1, 2 * hd), const)],
        out_specs=pl.BlockSpec((tq, vb.shape[1]), row),
        out_shape=jax.ShapeDtypeStruct((n, vb.shape[1]), BF16),
        scratch_shapes=[pltpu.VMEM((2 * B_HEADS, seq, hd), BF16)],
        compiler_params=_cparams("arbitrary", "arbitrary"),
        name="attn_b",
    )(qb, kb, vb, cos, sin, cos, sin, lq1, lk1, lq2, lk2, subln)


def _matmul_kernel(x_ref, w_ref, o_ref):
    o_ref[...] = jnp.dot(x_ref[...].astype(BF16), w_ref[...],
                         preferred_element_type=F32).astype(o_ref.dtype)


def _mem_kv(mem2d, w_bf, tm):
    n, d = mem2d.shape
    return pl.pallas_call(
        _matmul_kernel,
        grid=(n // tm,),
        in_specs=[pl.BlockSpec((tm, d), lambda i: (i, 0)),
                  pl.BlockSpec(w_bf.shape, lambda i: (0, 0))],
        out_specs=pl.BlockSpec((tm, w_bf.shape[1]), lambda i: (i, 0)),
        out_shape=jax.ShapeDtypeStruct((n, w_bf.shape[1]), BF16),
        compiler_params=_cparams("parallel"),
        name="mem_kv",
    )(mem2d, w_bf)


def _mix_xattn_kernel(oa_ref, ob_ref, x_ref, wmix_ref, g1_ref, b1_ref, wq_ref, kv_ref, wo_ref,
                      g2_ref, b2_ref, wr_ref, br_ref,
                      x2t_ref, code_ref, gate_ref, cnt_ref, carry_ref, *, alpha, n_experts):
    @pl.when(pl.program_id(0) == 0)
    def _():
        carry_ref[...] = jnp.zeros(carry_ref.shape, carry_ref.dtype)

    da = oa_ref.shape[1]
    mix = (jnp.dot(oa_ref[...], wmix_ref[:da, :], preferred_element_type=F32)
           + jnp.dot(ob_ref[...], wmix_ref[da:, :], preferred_element_type=F32))
    x1 = _layer_norm(alpha * x_ref[...] + mix, g1_ref[...], b1_ref[...])

    d = x1.shape[1]
    hd = d // MEM_HEADS
    q = jnp.dot(x1.astype(BF16), wq_ref[...], preferred_element_type=F32).astype(BF16)
    scale = hd ** -0.5
    outs = []
    for h in range(MEM_HEADS):
        s = lax.dot_general(q[:, h * hd:(h + 1) * hd], kv_ref[:, h * hd:(h + 1) * hd],
                            (((1,), (1,)), ((), ())), preferred_element_type=F32) * scale
        e, l = _softmax_parts(s)
        o = jnp.dot(e.astype(BF16), kv_ref[:, d + h * hd:d + (h + 1) * hd],
                    preferred_element_type=F32)
        outs.append((o * (1.0 / l)).astype(BF16))
    xa = jnp.dot(jnp.concatenate(outs, axis=-1), wo_ref[...], preferred_element_type=F32)
    x2 = _layer_norm(alpha * x1 + xa, g2_ref[...], b2_ref[...])
    tm = x2.shape[0]
    for j in range(SUBLANES):
        x2t_ref[pl.ds(j, tm, stride=SUBLANES), :] = x2[:, j * LANES:(j + 1) * LANES]

    logits = jnp.dot(x2.astype(BF16), wr_ref[...], preferred_element_type=F32) + br_ref[...]
    col = lax.broadcasted_iota(jnp.int32, logits.shape, 1).astype(F32)
    lane = lax.broadcasted_iota(jnp.int32, code_ref.shape, 1)
    vals, idxs = [], []
    for _ in range(TOP_K):
        m = jnp.max(logits, axis=-1, keepdims=True)
        idx = jnp.min(jnp.where(logits == m, col, float(n_experts)), axis=-1, keepdims=True)
        vals.append(m)
        idxs.append(idx)
        logits = jnp.where(col == idx, -jnp.inf, logits)
    es = [jnp.exp(v - vals[0]) for v in vals]
    den = es[0] + es[1] + es[2] + es[3]

    chosen = [col == idx for idx in idxs]
    cnt = sum(c.astype(F32) for c in chosen)
    r_id = lax.broadcasted_iota(jnp.int32, (tm, tm), 0)
    c_id = lax.broadcasted_iota(jnp.int32, (tm, tm), 1)
    lower = jnp.where(c_id < r_id, 1.0, 0.0).astype(BF16)
    before = carry_ref[...] + jnp.dot(lower, cnt.astype(BF16), preferred_element_type=F32)
    carry = carry_ref[...] + jnp.sum(cnt, axis=0, keepdims=True)
    carry_ref[...] = carry
    cnt_ref[...] = carry

    code_out = jnp.zeros(code_ref.shape, jnp.int32)
    gate_out = jnp.zeros(gate_ref.shape, F32)
    for k in range(TOP_K):
        rank = jnp.sum(jnp.where(chosen[k], before, 0.0), axis=-1, keepdims=True)
        code = idxs[k].astype(jnp.int32) * CODE_EXPERT_STRIDE + rank.astype(jnp.int32)
        code_out = jnp.where(lane == k, code, code_out)
        gate_out = jnp.where(lane == k, es[k] / den, gate_out)
    code_ref[...] = code_out
    gate_ref[...] = gate_out


def _mix_xattn(oa, ob, x2d, wmix, g1, b1, wq, kvmem, wo, g2, b2, wr, br, seq, mem_tokens, tm, alpha):
    n, d = x2d.shape
    tiles_per_b = seq // tm
    row = lambda i: (i, 0)
    const = lambda i: (0, 0)
    full = lambda a: pl.BlockSpec(a.shape, const)
    n_experts = wr.shape[1]
    assert d == SUBLANES * LANES and n < CODE_EXPERT_STRIDE
    return pl.pallas_call(
        functools.partial(_mix_xattn_kernel, alpha=alpha, n_experts=n_experts),
        grid=(n // tm,),
        in_specs=[pl.BlockSpec((tm, oa.shape[1]), row),
                  pl.BlockSpec((tm, ob.shape[1]), row),
                  pl.BlockSpec((tm, d), row),
                  full(wmix), full(g1), full(b1), full(wq),
                  pl.BlockSpec((mem_tokens, kvmem.shape[1]), lambda i: (i // tiles_per_b, 0)),
                  full(wo), full(g2), full(b2), full(wr), full(br)],
        out_specs=[pl.BlockSpec((tm * SUBLANES, LANES), row),
                   pl.BlockSpec((tm, LANES), row), pl.BlockSpec((tm, LANES), row),
                   pl.BlockSpec((1, n_experts), const)],
        out_shape=[jax.ShapeDtypeStruct((n * SUBLANES, LANES), F32),
                   jax.ShapeDtypeStruct((n, LANES), jnp.int32),
                   jax.ShapeDtypeStruct((n, LANES), F32),
                   jax.ShapeDtypeStruct((1, n_experts), F32)],
        scratch_shapes=[pltpu.VMEM((1, n_experts), F32)],
        compiler_params=_cparams("arbitrary"),
        name="mix_xattn",
    )(oa, ob, x2d, wmix, g1, b1, wq, kvmem, wo, g2, b2, wr, br)


def _tile_map(counts, tg, n_tiles):
    n_experts = counts.shape[0]
    padded = ((counts + tg - 1) // tg) * tg
    pad_end = jnp.cumsum(padded)
    pad_start = pad_end - padded
    tile_start = jnp.arange(n_tiles, dtype=jnp.int32) * tg
    tile_expert = jnp.minimum(
        jnp.sum((tile_start[:, None] >= pad_end[None, :]).astype(jnp.int32), axis=1), n_experts - 1)
    onehot = (tile_expert[:, None] == jnp.arange(n_experts, dtype=jnp.int32)[None, :]).astype(jnp.int32)
    group_end = jnp.sum(onehot * (pad_start + counts)[None, :], axis=1)
    tile_valid = jnp.clip(group_end - tile_start, 0, tg)
    n_used = (pad_end[-1] // tg).reshape(1)
    return (pad_start.astype(jnp.int32), tile_expert.astype(jnp.int32), tile_valid.astype(jnp.int32),
            n_used.astype(jnp.int32))


def _slot_of(code, start_ref):
    expert = lax.shift_right_logical(code, CODE_EXPERT_SHIFT)
    return start_ref[expert] + (code & (CODE_EXPERT_STRIDE - 1))


def _token_tile(ref, index):
    return ref.at[pl.ds(pl.multiple_of(index * SUBLANES, SUBLANES), SUBLANES)]


def _wait_tiles(src_ref, dst_ref, sem, count):
    def wait(_, c):
        pltpu.make_async_copy(_token_tile(src_ref, 0), _token_tile(dst_ref, 0), sem).wait()
        return c
    lax.fori_loop(0, count, wait, 0, unroll=8)


def _load_token_rows(ref, rows):
    return jnp.concatenate([ref[pl.ds(j, rows, stride=SUBLANES), :] for j in range(SUBLANES)], axis=-1)


def _dispatch_kernel(start_ref, code_ref, x_ref, o_hbm, sem, *, tokens):
    def start(t, c):
        for k in range(TOP_K):
            slot = _slot_of(code_ref[0, 0, t * TOP_K + k], start_ref)
            pltpu.make_async_copy(_token_tile(x_ref, t), _token_tile(o_hbm, slot), sem).start()
        return c
    lax.fori_loop(0, tokens, start, 0, unroll=2)
    _wait_tiles(x_ref, o_hbm, sem, tokens * TOP_K)


def _dispatch(x2t, code4, pad_start, n_slots, tm):
    n = x2t.shape[0] // SUBLANES
    nt = n // tm
    grid_spec = pltpu.PrefetchScalarGridSpec(
        num_scalar_prefetch=1,
        grid=(nt,),
        in_specs=[pl.BlockSpec((1, 1, tm * TOP_K), lambda i, ps: (i, 0, 0), memory_space=pltpu.SMEM),
                  pl.BlockSpec((tm * SUBLANES, LANES), lambda i, ps: (i, 0))],
        out_specs=pl.BlockSpec(memory_space=pl.ANY),
        scratch_shapes=[pltpu.SemaphoreType.DMA(())],
    )
    return pl.pallas_call(
        functools.partial(_dispatch_kernel, tokens=tm),
        grid_spec=grid_spec,
        out_shape=jax.ShapeDtypeStruct((n_slots * SUBLANES, LANES), x2t.dtype),
        compiler_params=_cparams("arbitrary"),
        name="dispatch",
    )(pad_start, code4.reshape(nt, 1, tm * TOP_K), x2t)


def _experts_kernel(te_ref, tv_ref, nused_ref, x_ref, wg_ref, bg_ref, wu_ref, bu_ref, wd_ref, bd_ref,
                    o_ref, wgb_ref, wub_ref, wdb_ref):
    i = pl.program_id(0)
    prev = te_ref[jnp.maximum(i - 1, 0)]
    used = i < nused_ref[0]
    tg = x_ref.shape[0] // SUBLANES

    @pl.when(used & ((i == 0) | (te_ref[i] != prev)))
    def _():
        wgb_ref[...] = wg_ref[0].astype(BF16)
        wub_ref[...] = wu_ref[0].astype(BF16)
        wdb_ref[...] = wd_ref[0].astype(BF16)

    @pl.when(used)
    def _():
        row = lax.broadcasted_iota(jnp.int32, (tg, 1), 0)
        x = jnp.where(row < tv_ref[i], _load_token_rows(x_ref, tg), 0.0).astype(BF16)
        g = jnp.dot(x, wgb_ref[...], preferred_element_type=F32) + bg_ref[0]
        u = jnp.dot(x, wub_ref[...], preferred_element_type=F32) + bu_ref[0]
        g = jnp.minimum(g, SWIGLU_LIMIT)
        u = jnp.clip(u, -SWIGLU_LIMIT, SWIGLU_LIMIT)
        h = g * (1.0 / (1.0 + jnp.exp(-SWIGLU_ALPHA * g))) * (u + 1.0)
        y = jnp.dot(h.astype(BF16), wdb_ref[...], preferred_element_type=F32) + bd_ref[0]
        for j in range(SUBLANES):
            o_ref[pl.ds(j, tg, stride=SUBLANES), :] = y[:, j * LANES:(j + 1) * LANES]

    @pl.when(jnp.logical_not(used))
    def _():
        o_ref[...] = jnp.zeros(o_ref.shape, o_ref.dtype)


def _experts(xs, tile_expert, tile_valid, n_used, wg, bg, wu, bu, wd, bd, tg):
    e, d, f = wg.shape
    n_tiles = xs.shape[0] // (tg * SUBLANES)
    xrow = lambda i, te, tv, nu: (jnp.minimum(i, nu[0] - 1), 0)
    wmap = lambda i, te, tv, nu: (te[i], 0, 0)
    grid_spec = pltpu.PrefetchScalarGridSpec(
        num_scalar_prefetch=3,
        grid=(n_tiles,),
        in_specs=[pl.BlockSpec((tg * SUBLANES, LANES), xrow),
                  pl.BlockSpec((1, d, f), wmap), pl.BlockSpec((1, 1, f), wmap),
                  pl.BlockSpec((1, d, f), wmap), pl.BlockSpec((1, 1, f), wmap),
                  pl.BlockSpec((1, f, d), wmap), pl.BlockSpec((1, 1, d), wmap)],
        out_specs=pl.BlockSpec((tg * SUBLANES, LANES), lambda i, te, tv, nu: (i, 0)),
        scratch_shapes=[pltpu.VMEM((d, f), BF16), pltpu.VMEM((d, f), BF16), pltpu.VMEM((f, d), BF16)],
    )
    return pl.pallas_call(
        _experts_kernel,
        grid_spec=grid_spec,
        out_shape=jax.ShapeDtypeStruct(xs.shape, F32),
        compiler_params=_cparams("arbitrary"),
        name="experts",
    )(tile_expert, tile_valid, n_used, xs, wg, bg.reshape(e, 1, f), wu, bu.reshape(e, 1, f),
      wd, bd.reshape(e, 1, d))


def _combine_kernel(start_ref, code_ref, ys_hbm, x2t_ref, gate_ref, g_ref, b_ref, o_ref, buf_ref, sem,
                    *, alpha):
    tc = gate_ref.shape[0]

    def start(t, c):
        for k in range(TOP_K):
            slot = _slot_of(code_ref[0, 0, t * TOP_K + k], start_ref)
            pltpu.make_async_copy(_token_tile(ys_hbm, slot), _token_tile(buf_ref.at[k], t), sem).start()
        return c
    lax.fori_loop(0, tc, start, 0, unroll=2)
    _wait_tiles(ys_hbm, buf_ref.at[0], sem, tc * TOP_K)

    gates = gate_ref[...]
    y = alpha * _load_token_rows(x2t_ref, tc)
    for k in range(TOP_K):
        y = y + gates[:, k:k + 1] * _load_token_rows(buf_ref.at[k], tc)
    o_ref[...] = _layer_norm(y, g_ref[...], b_ref[...])


def _combine(ys, code4, pad_start, x2t, gates, g3, b3, tc, alpha):
    n = x2t.shape[0] // SUBLANES
    d = SUBLANES * LANES
    nt = n // tc
    const = lambda i, ps: (0, 0)
    grid_spec = pltpu.PrefetchScalarGridSpec(
        num_scalar_prefetch=1,
        grid=(nt,),
        in_specs=[pl.BlockSpec((1, 1, tc * TOP_K), lambda i, ps: (i, 0, 0), memory_space=pltpu.SMEM),
                  pl.BlockSpec(memory_space=pl.ANY),
                  pl.BlockSpec((tc * SUBLANES, LANES), lambda i, ps: (i, 0)),
                  pl.BlockSpec((tc, LANES), lambda i, ps: (i, 0)),
                  pl.BlockSpec(g3.shape, const), pl.BlockSpec(b3.shape, const)],
        out_specs=pl.BlockSpec((tc, d), lambda i, ps: (i, 0)),
        scratch_shapes=[pltpu.VMEM((TOP_K, tc * SUBLANES, LANES), ys.dtype), pltpu.SemaphoreType.DMA(())],
    )
    return pl.pallas_call(
        functools.partial(_combine_kernel, alpha=alpha),
        grid_spec=grid_spec,
        out_shape=jax.ShapeDtypeStruct((n, d), F32),
        compiler_params=_cparams("arbitrary"),
        name="combine",
    )(pad_start, code4.reshape(nt, 1, tc * TOP_K), ys, x2t, gates, g3, b3)


def _rope_tables(seq, a_hd, b_hd):
    def cos_sin(pos, dim):
        inv = ROPE_THETA ** (-jnp.arange(0, dim, 2, dtype=F32) / dim)
        ang = pos.astype(F32)[:, None] * inv[None, :]
        return jnp.cos(ang), jnp.sin(ang)

    t = jnp.arange(seq, dtype=jnp.int32)
    cr, sr = cos_sin(t // GRID_W, a_hd // 2)
    cc, sc = cos_sin(t % GRID_W, a_hd // 2)
    cs, ss = cos_sin(t, b_hd)
    cos_a = jnp.concatenate([cr, cr, cc, cc], axis=-1)
    sin_a = jnp.concatenate([-sr, sr, -sc, sc], axis=-1)
    cos_b = jnp.concatenate([cs, cs], axis=-1)
    sin_b = jnp.concatenate([-ss, ss], axis=-1)
    return cos_a, sin_a, cos_b, sin_b


def _pick_tile(n, pref):
    t = min(pref, n)
    while n % t:
        t //= 2
    return t


def kernel(x, mem, w_in, a_q_norm, a_k_norm, b_lambda_q1, b_lambda_k1, b_lambda_q2, b_lambda_k2, b_subln, w_mix_out, ln1_g, ln1_b, w_mem_q, w_mem_kv, w_mem_out, ln2_g, ln2_b, w_router, b_router, w_e_gate, b_e_gate, w_e_up, b_e_up, w_e_down, b_e_down, ln3_g, ln3_b):
    batch, seq, d = x.shape
    depth = w_in.shape[0]
    n = batch * seq
    mem_tokens = mem.shape[1]
    n_experts = w_router.shape[-1]
    a_hd = a_q_norm.shape[-1]
    b_hd = b_lambda_q1.shape[-1]
    alpha = (2.0 * depth) ** 0.25

    a_q, a_kv = A_HEADS * a_hd, A_KV_HEADS * a_hd
    b_qk, b_v = B_HEADS * 2 * b_hd, B_HEADS * 2 * b_hd
    edges = [0, a_q, a_q + a_kv, a_q + 2 * a_kv, a_q + 2 * a_kv + b_qk,
             a_q + 2 * a_kv + 2 * b_qk, a_q + 2 * a_kv + 2 * b_qk + b_v]
    splits = tuple(zip(edges[:-1], edges[1:]))
    assert edges[-1] == w_in.shape[-1]

    tm = _pick_tile(seq, 512)
    tq = _pick_tile(seq, 256)
    tg = 512
    tc = _pick_tile(n, 256)
    n_tiles = (n * TOP_K) // tg + n_experts
    cos_a, sin_a, cos_b, sin_b = _rope_tables(seq, a_hd, b_hd)
    row2 = lambda v: v.reshape(1, -1)

    h = x.reshape(n, d)
    mem2d = mem.reshape(batch * mem_tokens, d)
    for layer in range(depth):
        lambda_init = 0.8 - 0.6 * math.exp(-0.3 * layer)
        qa, ka, va, qb, kb, vb = _in_proj(h, w_in[layer].astype(BF16), splits, tm)
        oa = _attn_a(qa, ka, va, cos_a, sin_a, row2(a_q_norm[layer]), row2(a_k_norm[layer]),
                     batch, seq, tq)
        ob = _attn_b(qb, kb, vb, cos_b, sin_b, row2(b_lambda_q1[layer]), row2(b_lambda_k1[layer]),
                     row2(b_lambda_q2[layer]), row2(b_lambda_k2[layer]), row2(b_subln[layer]),
                     batch, seq, tq, lambda_init)
        kvmem = _mem_kv(mem2d, w_mem_kv[layer].astype(BF16), _pick_tile(batch * mem_tokens, 512))
        x2t, code, gates, counts = _mix_xattn(
            oa, ob, h, w_mix_out[layer].astype(BF16), row2(ln1_g[layer]), row2(ln1_b[layer]),
            w_mem_q[layer].astype(BF16), kvmem, w_mem_out[layer].astype(BF16),
            row2(ln2_g[layer]), row2(ln2_b[layer]), w_router[layer].astype(BF16),
            row2(b_router[layer]), seq, mem_tokens, tm, alpha)
        pad_start, tile_expert, tile_valid, n_used = _tile_map(
            counts.reshape(-1).astype(jnp.int32), tg, n_tiles)
        code4 = code[:, :TOP_K]
        xs = _dispatch(x2t, code4, pad_start, n_tiles * tg, tm)
        ys = _experts(xs, tile_expert, tile_valid, n_used, w_e_gate[layer], b_e_gate[layer],
                      w_e_up[layer], b_e_up[layer], w_e_down[layer], b_e_down[layer], tg)
        h = _combine(ys, code4, pad_start, x2t, gates, row2(ln3_g[layer]), row2(ln3_b[layer]), tc, alpha)
    return h.reshape(batch, seq, d)
```

```python
import functools
import math

import jax
import jax.numpy as jnp
from jax import lax
from jax.experimental import pallas as pl
from jax.experimental.pallas import tpu as pltpu

F32 = jnp.float32
BF16 = jnp.bfloat16

GRID_W = 64
ROPE_THETA = 10000.0
A_HEADS = 8
A_KV_HEADS = 2
A_GROUP = A_HEADS // A_KV_HEADS
B_HEADS = 4
MEM_HEADS = 4
TOP_K = 4
SWIGLU_LIMIT = 7.0
SWIGLU_ALPHA = 1.702
LN_EPS = 1e-5
RMS_EPS = 1e-6
LOG2E = 1.4426950408889634

VMEM_LIMIT_BYTES = 56 * 1024 * 1024
LANES = 128
SUBLANES = 8
CODE_EXPERT_SHIFT = 20
CODE_EXPERT_STRIDE = 1 << CODE_EXPERT_SHIFT
NT_DIMS = (((1,), (1,)), ((), ()))


def _cparams(*sem):
    return pltpu.CompilerParams(dimension_semantics=sem, vmem_limit_bytes=VMEM_LIMIT_BYTES)


def _layer_norm(x, g, b):
    mu = jnp.mean(x, axis=-1, keepdims=True)
    xc = x - mu
    var = jnp.mean(xc * xc, axis=-1, keepdims=True)
    return xc * lax.rsqrt(var + LN_EPS) * g + b


def _rms_norm(x, g):
    return x * lax.rsqrt(jnp.mean(x * x, axis=-1, keepdims=True) + RMS_EPS) * g


def _rope(x, cos, sin_signed, half):
    w = x.shape[-1]
    lane = lax.broadcasted_iota(jnp.int32, (1, w), 1)
    first = (lane & (2 * half - 1)) < half
    partner = jnp.where(first, pltpu.roll(x, w - half, 1), pltpu.roll(x, half, 1))
    return x * cos + partner * sin_signed


def _segment_mean(x, bd):
    hi = x.astype(BF16)
    lo = (x - hi.astype(F32)).astype(BF16)
    return (jnp.dot(hi, bd, preferred_element_type=F32) + jnp.dot(lo, bd, preferred_element_type=F32))


def _swap_lane_halves(x):
    return pltpu.roll(x, x.shape[-1] // 2, 1)


def _in_proj_kernel(x_ref, w_ref, *out_refs, splits):
    xb = x_ref[...].astype(BF16)
    for o_ref, (c0, c1) in zip(out_refs, splits):
        o_ref[...] = jnp.dot(xb, w_ref[:, c0:c1], preferred_element_type=F32).astype(o_ref.dtype)


def _in_proj(x2d, w_bf, splits, tm):
    n, d = x2d.shape
    widths = [c1 - c0 for c0, c1 in splits]
    return pl.pallas_call(
        functools.partial(_in_proj_kernel, splits=splits),
        grid=(n // tm,),
        in_specs=[pl.BlockSpec((tm, d), lambda i: (i, 0)),
                  pl.BlockSpec(w_bf.shape, lambda i: (0, 0))],
        out_specs=[pl.BlockSpec((tm, w), lambda i: (i, 0)) for w in widths],
        out_shape=[jax.ShapeDtypeStruct((n, w), BF16) for w in widths],
        compiler_params=_cparams("parallel"),
        name="in_proj",
    )(x2d, w_bf)


def _attn_a_kernel(q_ref, k_ref, v_ref, cq_ref, sq_ref, ck_ref, sk_ref, gq_ref, gk_ref, bd_ref,
                   o_ref, kn_ref, va_ref, *, hd):
    low = lax.broadcasted_iota(jnp.int32, (1, 2 * hd), 1) < hd

    @pl.when(pl.program_id(1) == 0)
    def _():
        k = k_ref[...].astype(F32)
        k = k * lax.rsqrt(_segment_mean(k * k, bd_ref[...]) + RMS_EPS) * gk_ref[...]
        k = _rope(k, ck_ref[...], sk_ref[...], hd // 4)
        k_sw = _swap_lane_halves(k)
        v = v_ref[...].astype(F32)
        kn_ref[0] = jnp.where(low, k, k_sw).astype(BF16)
        kn_ref[1] = jnp.where(low, k_sw, k).astype(BF16)
        va_ref[0] = jnp.where(low, v, 1.0).astype(BF16)
        va_ref[1] = jnp.where(low, _swap_lane_halves(v), 1.0).astype(BF16)

    scale = hd ** -0.5 * LOG2E

    n_pairs = A_HEADS // 2
    qs = []
    for pair in range(n_pairs):
        q = q_ref[:, pair * 2 * hd:(pair + 1) * 2 * hd].astype(F32)
        q = q * lax.rsqrt(_segment_mean(q * q, bd_ref[...]) + RMS_EPS) * gq_ref[...]
        q = _rope(q, cq_ref[...], sq_ref[...], hd // 4) * scale
        qs.append([jnp.where(keep, q, 0.0).astype(BF16) for keep in (low, jnp.logical_not(low))])

    def scores(pair):
        kv = (2 * pair) // A_GROUP
        return [lax.dot_general(qh, kn_ref[kv], NT_DIMS, preferred_element_type=F32) for qh in qs[pair]]

    outs = []
    s_next = scores(0)
    for pair in range(n_pairs):
        s_cur = s_next
        if pair + 1 < n_pairs:
            s_next = scores(pair + 1)
        kv = (2 * pair) // A_GROUP
        o_pair = []
        for s in s_cur:
            e = jnp.exp2(s - jnp.max(s, axis=-1, keepdims=True)).astype(BF16)
            oa = jnp.dot(e, va_ref[kv], preferred_element_type=F32)
            l = jnp.where(low, _swap_lane_halves(oa), oa)
            o_pair.append(oa * (1.0 / l))
        outs.append(jnp.where(low, o_pair[0], _swap_lane_halves(o_pair[1])))
    o_ref[...] = jnp.concatenate(outs, axis=-1).astype(o_ref.dtype)


def _attn_a(qa, ka, va, cos, sin, gq, gk, bd, batch, seq, tq):
    n, dq = qa.shape
    hd = dq // A_HEADS
    assert 2 * hd == LANES and ka.shape[1] == LANES and va.shape[1] == LANES
    nq = seq // tq
    row = lambda b, i: (b * nq + i, 0)
    per_b = lambda b, i: (b, 0)
    const = lambda b, i: (0, 0)
    return pl.pallas_call(
        functools.partial(_attn_a_kernel, hd=hd),
        grid=(batch, nq),
        in_specs=[pl.BlockSpec((tq, dq), row),
                  pl.BlockSpec((seq, LANES), per_b),
                  pl.BlockSpec((seq, LANES), per_b),
                  pl.BlockSpec((tq, LANES), lambda b, i: (i, 0)),
                  pl.BlockSpec((tq, LANES), lambda b, i: (i, 0)),
                  pl.BlockSpec((seq, LANES), const),
                  pl.BlockSpec((seq, LANES), const),
                  pl.BlockSpec((1, LANES), const),
                  pl.BlockSpec((1, LANES), const),
                  pl.BlockSpec((LANES, LANES), const)],
        out_specs=pl.BlockSpec((tq, dq), row),
        out_shape=jax.ShapeDtypeStruct((n, dq), BF16),
        scratch_shapes=[pltpu.VMEM((A_KV_HEADS, seq, LANES), BF16),
                        pltpu.VMEM((A_KV_HEADS, seq, LANES), BF16)],
        compiler_params=_cparams("arbitrary", "arbitrary"),
        name="attn_a",
    )(qa, ka, va, cos, sin, cos, sin, gq, gk, bd)


def _attn_b_kernel(q_ref, k_ref, v_ref, cq_ref, sq_ref, ck_ref, sk_ref,
                   lq1_ref, lk1_ref, lq2_ref, lk2_ref, g_ref, o_ref, kr_ref, va_ref, *, hd, lambda_init):
    low = lax.broadcasted_iota(jnp.int32, (1, 2 * hd), 1) < hd
    vd = 2 * hd

    @pl.when(pl.program_id(1) == 0)
    def _():
        ones = jnp.ones((v_ref.shape[0], vd), BF16)
        for h in range(B_HEADS):
            k = _rope(k_ref[:, h * 2 * hd:(h + 1) * 2 * hd].astype(F32), ck_ref[...], sk_ref[...], hd // 2)
            k_sw = _swap_lane_halves(k)
            kr_ref[2 * h] = jnp.where(low, k, k_sw).astype(BF16)
            kr_ref[2 * h + 1] = jnp.where(low, k_sw, k).astype(BF16)
            va_ref[h] = jnp.concatenate([v_ref[:, h * vd:(h + 1) * vd], ones], axis=-1)

    lam = (jnp.exp(jnp.sum(lq1_ref[...] * lk1_ref[...], axis=-1, keepdims=True))
           - jnp.exp(jnp.sum(lq2_ref[...] * lk2_ref[...], axis=-1, keepdims=True))
           + lambda_init)
    scale = hd ** -0.5 * LOG2E
    qs = []
    for h in range(B_HEADS):
        q = q_ref[:, h * 2 * hd:(h + 1) * 2 * hd].astype(F32)
        q = _rope(q, cq_ref[...], sq_ref[...], hd // 2) * scale
        qs.append([jnp.where(keep, q, 0.0).astype(BF16) for keep in (low, jnp.logical_not(low))])

    def scores(h):
        return [lax.dot_general(qm, kr_ref[2 * h + c], NT_DIMS, preferred_element_type=F32)
                for c, qm in enumerate(qs[h])]

    outs = []
    s_next = scores(0)
    for h in range(B_HEADS):
        s_cur = s_next
        if h + 1 < B_HEADS:
            s_next = scores(h + 1)
        oa = [jnp.dot(jnp.exp2(s - jnp.max(s, axis=-1, keepdims=True)).astype(BF16), va_ref[h],
                      preferred_element_type=F32) for s in s_cur]
        o = oa[0][:, :vd] * (1.0 / oa[0][:, vd:]) - oa[1][:, :vd] * (lam / oa[1][:, vd:])
        outs.append(_rms_norm(o, g_ref[...]) * (1.0 - lambda_init))
    o_ref[...] = jnp.concatenate(outs, axis=-1).astype(o_ref.dtype)


def _attn_b(qb, kb, vb, cos, sin, lq1, lk1, lq2, lk2, subln, batch, seq, tq, lambda_init):
    n, dq = qb.shape
    hd = dq // (2 * B_HEADS)
    assert 2 * hd == LANES and vb.shape[1] == B_HEADS * LANES
    nq = seq // tq
    row = lambda b, i: (b * nq + i, 0)
    per_b = lambda b, i: (b, 0)
    const = lambda b, i: (0, 0)
    vec = pl.BlockSpec((1, hd), const)
    return pl.pallas_call(
        functools.partial(_attn_b_kernel, hd=hd, lambda_init=lambda_init),
        grid=(batch, nq),
        in_specs=[pl.BlockSpec((tq, dq), row),
                  pl.BlockSpec((seq, dq), per_b),
                  pl.BlockSpec((seq, vb.shape[1]), per_b),
                  pl.BlockSpec((tq, LANES), lambda b, i: (i, 0)),
                  pl.BlockSpec((tq, LANES), lambda b, i: (i, 0)),
                  pl.BlockSpec((seq, LANES), const),
                  pl.BlockSpec((seq, LANES), const),
                  vec, vec, vec, vec,
                  pl.BlockSpec((1, LANES), const)],
        out_specs=pl.BlockSpec((tq, vb.shape[1]), row),
        out_shape=jax.ShapeDtypeStruct((n, vb.shape[1]), BF16),
        scratch_shapes=[pltpu.VMEM((2 * B_HEADS, seq, LANES), BF16),
                        pltpu.VMEM((B_HEADS, seq, 2 * LANES), BF16)],
        compiler_params=_cparams("arbitrary", "arbitrary"),
        name="attn_b",
    )(qb, kb, vb, cos, sin, cos, sin, lq1, lk1, lq2, lk2, subln)


def _matmul_kernel(x_ref, w_ref, o_ref):
    o_ref[...] = jnp.dot(x_ref[...].astype(BF16), w_ref[...],
                         preferred_element_type=F32).astype(o_ref.dtype)


def _mem_kv(mem2d, w_bf, tm):
    n, d = mem2d.shape
    return pl.pallas_call(
        _matmul_kernel,
        grid=(n // tm,),
        in_specs=[pl.BlockSpec((tm, d), lambda i: (i, 0)),
                  pl.BlockSpec(w_bf.shape, lambda i: (0, 0))],
        out_specs=pl.BlockSpec((tm, w_bf.shape[1]), lambda i: (i, 0)),
        out_shape=jax.ShapeDtypeStruct((n, w_bf.shape[1]), BF16),
        compiler_params=_cparams("parallel"),
        name="mem_kv",
    )(mem2d, w_bf)


def _mix_xattn_kernel(oa_ref, ob_ref, x_ref, wmix_ref, g1_ref, b1_ref, wq_ref, kv_ref, wo_ref,
                      g2_ref, b2_ref, wr_ref, br_ref,
                      x2t_ref, code_ref, gate_ref, cnt_ref, carry_ref, *, alpha, n_experts, sub):
    @pl.when(pl.program_id(0) == 0)
    def _():
        carry_ref[...] = jnp.zeros(carry_ref.shape, carry_ref.dtype)

    da = oa_ref.shape[1]
    d = x_ref.shape[1]
    hd = d // MEM_HEADS
    scale = hd ** -0.5
    starts = range(0, x_ref.shape[0], sub)
    subs = [pl.ds(r0, sub) for r0 in starts]

    mix = [jnp.dot(oa_ref[rows, :], wmix_ref[:da, :], preferred_element_type=F32)
           + jnp.dot(ob_ref[rows, :], wmix_ref[da:, :], preferred_element_type=F32) for rows in subs]
    x1 = [_layer_norm(alpha * x_ref[rows, :] + m, g1_ref[...], b1_ref[...]) for rows, m in zip(subs, mix)]
    q = [jnp.dot(v.astype(BF16), wq_ref[...], preferred_element_type=F32).astype(BF16) for v in x1]

    def mem_attention(qv):
        outs = []
        for h in range(MEM_HEADS):
            s = lax.dot_general(qv[:, h * hd:(h + 1) * hd], kv_ref[:, h * hd:(h + 1) * hd],
                                NT_DIMS, preferred_element_type=F32) * scale
            e = jnp.exp(s - jnp.max(s, axis=-1, keepdims=True))
            l = jnp.sum(e, axis=-1, keepdims=True)
            o = jnp.dot(e.astype(BF16), kv_ref[:, d + h * hd:d + (h + 1) * hd],
                        preferred_element_type=F32)
            outs.append((o * (1.0 / l)).astype(BF16))
        return jnp.concatenate(outs, axis=-1)

    att = [mem_attention(qv) for qv in q]
    xa = [jnp.dot(a, wo_ref[...], preferred_element_type=F32) for a in att]
    x2 = [_layer_norm(alpha * v + a, g2_ref[...], b2_ref[...]) for v, a in zip(x1, xa)]
    for r0, v in zip(starts, x2):
        for j in range(SUBLANES):
            x2t_ref[pl.ds(r0 * SUBLANES + j, sub, stride=SUBLANES), :] = v[:, j * LANES:(j + 1) * LANES]
    all_logits = [jnp.dot(v.astype(BF16), wr_ref[...], preferred_element_type=F32) + br_ref[...] for v in x2]

    col = lax.broadcasted_iota(jnp.int32, (sub, n_experts), 1).astype(F32)
    lane = lax.broadcasted_iota(jnp.int32, (sub, LANES), 1)
    r_id = lax.broadcasted_iota(jnp.int32, (sub, sub), 0)
    c_id = lax.broadcasted_iota(jnp.int32, (sub, sub), 1)
    lower = jnp.where(c_id < r_id, 1.0, 0.0).astype(BF16)
    for rows, logits in zip(subs, all_logits):
        vals, idxs = [], []
        for _ in range(TOP_K):
            m = jnp.max(logits, axis=-1, keepdims=True)
            idx = jnp.min(jnp.where(logits == m, col, float(n_experts)), axis=-1, keepdims=True)
            vals.append(m)
            idxs.append(idx)
            logits = jnp.where(col == idx, -jnp.inf, logits)
        es = [jnp.exp(v - vals[0]) for v in vals]
        den = es[0] + es[1] + es[2] + es[3]

        chosen = [col == idx for idx in idxs]
        cnt = sum(c.astype(F32) for c in chosen)
        before = carry_ref[...] + jnp.dot(lower, cnt.astype(BF16), preferred_element_type=F32)
        carry_ref[...] = carry_ref[...] + jnp.sum(cnt, axis=0, keepdims=True)

        code_out = jnp.zeros((sub, LANES), jnp.int32)
        gate_out = jnp.zeros((sub, LANES), F32)
        for k in range(TOP_K):
            rank = jnp.sum(jnp.where(chosen[k], before, 0.0), axis=-1, keepdims=True)
            code = idxs[k].astype(jnp.int32) * CODE_EXPERT_STRIDE + rank.astype(jnp.int32)
            code_out = jnp.where(lane == k, code, code_out)
            gate_out = jnp.where(lane == k, es[k] / den, gate_out)
        code_ref[rows, :] = code_out
        gate_ref[rows, :] = gate_out
    cnt_ref[...] = carry_ref[...]


def _mix_xattn(oa, ob, x2d, wmix, g1, b1, wq, kvmem, wo, g2, b2, wr, br, seq, mem_tokens, tm, alpha):
    n, d = x2d.shape
    tiles_per_b = seq // tm
    row = lambda i: (i, 0)
    const = lambda i: (0, 0)
    full = lambda a: pl.BlockSpec(a.shape, const)
    n_experts = wr.shape[1]
    assert d == SUBLANES * LANES and n < CODE_EXPERT_STRIDE
    return pl.pallas_call(
        functools.partial(_mix_xattn_kernel, alpha=alpha, n_experts=n_experts, sub=_pick_tile(tm, 256)),
        grid=(n // tm,),
        in_specs=[pl.BlockSpec((tm, oa.shape[1]), row),
                  pl.BlockSpec((tm, ob.shape[1]), row),
                  pl.BlockSpec((tm, d), row),
                  full(wmix), full(g1), full(b1), full(wq),
                  pl.BlockSpec((mem_tokens, kvmem.shape[1]), lambda i: (i // tiles_per_b, 0)),
                  full(wo), full(g2), full(b2), full(wr), full(br)],
        out_specs=[pl.BlockSpec((tm * SUBLANES, LANES), row),
                   pl.BlockSpec((tm, LANES), row), pl.BlockSpec((tm, LANES), row),
                   pl.BlockSpec((1, n_experts), const)],
        out_shape=[jax.ShapeDtypeStruct((n * SUBLANES, LANES), F32),
                   jax.ShapeDtypeStruct((n, LANES), jnp.int32),
                   jax.ShapeDtypeStruct((n, LANES), F32),
                   jax.ShapeDtypeStruct((1, n_experts), F32)],
        scratch_shapes=[pltpu.VMEM((1, n_experts), F32)],
        compiler_params=_cparams("arbitrary"),
        name="mix_xattn",
    )(oa, ob, x2d, wmix, g1, b1, wq, kvmem, wo, g2, b2, wr, br)


def _tile_map(counts, code4, tg, n_tiles):
    n_experts = counts.shape[0]
    experts = jnp.arange(n_experts, dtype=jnp.int32)
    padded = ((counts + tg - 1) // tg) * tg
    pad_end = jnp.cumsum(padded)
    pad_start = pad_end - padded
    n_used = pad_end[-1] // tg

    pair_expert = lax.shift_right_logical(code4, CODE_EXPERT_SHIFT)
    pair_start = jnp.sum(jnp.where(pair_expert[..., None] == experts, pad_start, 0), axis=-1)
    slots = pair_start + (code4 & (CODE_EXPERT_STRIDE - 1))

    tile_id = jnp.arange(n_tiles, dtype=jnp.int32)
    tile_start = tile_id * tg
    tile_expert = jnp.minimum(
        jnp.sum((tile_start[:, None] >= pad_end[None, :]).astype(jnp.int32), axis=1), n_experts - 1)
    onehot = (tile_expert[:, None] == experts[None, :]).astype(jnp.int32)
    pick = lambda per_expert: jnp.sum(onehot * per_expert[None, :], axis=1)
    tile_valid = jnp.clip(pick(pad_start + counts) - tile_start, 0, tg)
    next_tile = pick(pad_end) // tg
    has_next = (next_tile < n_used).astype(jnp.int32)
    next_expert = jnp.sum((tile_id[:, None] == jnp.minimum(next_tile, n_tiles - 1)[None, :]).astype(jnp.int32)
                          * tile_expert[:, None], axis=0)
    nonempty_before = jnp.cumsum((counts > 0).astype(jnp.int32)) - (counts > 0).astype(jnp.int32)
    tile_parity = pick(nonempty_before) & 1
    i32 = lambda a: a.astype(jnp.int32)
    return (i32(slots), i32(tile_expert), i32(tile_valid), i32(has_next), i32(next_expert),
            i32(tile_parity), i32(n_used).reshape(1))


def _token_tile(ref, index):
    return ref.at[pl.ds(pl.multiple_of(index * SUBLANES, SUBLANES), SUBLANES)]


def _wait_tiles(src_ref, dst_ref, sem, count):
    def wait(_, c):
        pltpu.make_async_copy(_token_tile(src_ref, 0), _token_tile(dst_ref, 0), sem).wait()
        return c
    lax.fori_loop(0, count, wait, 0, unroll=8)


def _load_token_rows(ref, rows):
    return jnp.concatenate([ref[pl.ds(j, rows, stride=SUBLANES), :] for j in range(SUBLANES)], axis=-1)


def _dispatch_kernel(slot_ref, x_ref, o_hbm, sem, *, tokens):
    def start(t, c):
        for k in range(TOP_K):
            pltpu.make_async_copy(_token_tile(x_ref, t), _token_tile(o_hbm, slot_ref[0, 0, t * TOP_K + k]),
                                  sem).start(priority=k % 2)
        return c
    lax.fori_loop(0, tokens, start, 0, unroll=2)
    _wait_tiles(x_ref, o_hbm, sem, tokens * TOP_K)


def _dispatch(x2t, slots, n_slots, tm):
    n = x2t.shape[0] // SUBLANES
    nt = n // tm
    return pl.pallas_call(
        functools.partial(_dispatch_kernel, tokens=tm),
        grid=(nt,),
        in_specs=[pl.BlockSpec((1, 1, tm * TOP_K), lambda i: (i, 0, 0), memory_space=pltpu.SMEM),
                  pl.BlockSpec((tm * SUBLANES, LANES), lambda i: (i, 0))],
        out_specs=pl.BlockSpec(memory_space=pl.ANY),
        out_shape=jax.ShapeDtypeStruct((n_slots * SUBLANES, LANES), x2t.dtype),
        scratch_shapes=[pltpu.SemaphoreType.DMA(())],
        compiler_params=_cparams("arbitrary"),
        name="dispatch",
    )(slots.reshape(nt, 1, tm * TOP_K), x2t)


def _experts_kernel(te_ref, tv_ref, hn_ref, ne_ref, par_ref, nused_ref,
                    x_ref, wg_hbm, bg_ref, wu_hbm, bu_ref, wd_hbm, bd_ref,
                    o_ref, wg_st, wu_st, wd_st, wgb_ref, wub_ref, wdb_ref, sem):
    i = pl.program_id(0)
    used = i < nused_ref[0]
    first = used & ((i == 0) | (te_ref[i] != te_ref[jnp.maximum(i - 1, 0)]))
    tg = x_ref.shape[0] // SUBLANES

    def weight_copies(expert, slot):
        return [pltpu.make_async_copy(hbm.at[expert], st.at[slot], sem.at[slot])
                for hbm, st in ((wg_hbm, wg_st), (wu_hbm, wu_st), (wd_hbm, wd_st))]

    @pl.when(used & (i == 0))
    def _():
        for c in weight_copies(te_ref[0], par_ref[0]):
            c.start()

    @pl.when(first)
    def _():
        slot = par_ref[i]
        for c in weight_copies(te_ref[i], slot):
            c.wait()

        @pl.when(hn_ref[i] == 1)
        def _():
            for c in weight_copies(ne_ref[i], 1 - slot):
                c.start()

        wgb_ref[...] = wg_st[slot].astype(BF16)
        wub_ref[...] = wu_st[slot].astype(BF16)
        wdb_ref[...] = wd_st[slot].astype(BF16)

    @pl.when(used)
    def _():
        row = lax.broadcasted_iota(jnp.int32, (tg, 1), 0)
        x = jnp.where(row < tv_ref[i], _load_token_rows(x_ref, tg), 0.0).astype(BF16)
        g = jnp.dot(x, wgb_ref[...], preferred_element_type=F32) + bg_ref[0]
        u = jnp.dot(x, wub_ref[...], preferred_element_type=F32) + bu_ref[0]
        g = jnp.minimum(g, SWIGLU_LIMIT)
        u = jnp.clip(u, -SWIGLU_LIMIT, SWIGLU_LIMIT)
        h = g * (1.0 / (1.0 + jnp.exp(-SWIGLU_ALPHA * g))) * (u + 1.0)
        y = jnp.dot(h.astype(BF16), wdb_ref[...], preferred_element_type=F32) + bd_ref[0]
        for j in range(SUBLANES):
            o_ref[pl.ds(j, tg, stride=SUBLANES), :] = y[:, j * LANES:(j + 1) * LANES]

    @pl.when(jnp.logical_not(used))
    def _():
        o_ref[...] = jnp.zeros(o_ref.shape, o_ref.dtype)


def _experts(xs, tile_expert, tile_valid, has_next, next_expert, tile_parity, n_used,
             wg, bg, wu, bu, wd, bd, tg):
    e, d, f = wg.shape
    n_tiles = xs.shape[0] // (tg * SUBLANES)
    xrow = lambda i, te, tv, hn, ne, par, nu: (jnp.minimum(i, nu[0] - 1), 0)
    bmap = lambda i, te, tv, hn, ne, par, nu: (te[i], 0, 0)
    hbm = pl.BlockSpec(memory_space=pl.ANY)
    grid_spec = pltpu.PrefetchScalarGridSpec(
        num_scalar_prefetch=6,
        grid=(n_tiles,),
        in_specs=[pl.BlockSpec((tg * SUBLANES, LANES), xrow),
                  hbm, pl.BlockSpec((1, 1, f), bmap),
                  hbm, pl.BlockSpec((1, 1, f), bmap),
                  hbm, pl.BlockSpec((1, 1, d), bmap)],
        out_specs=pl.BlockSpec((tg * SUBLANES, LANES), lambda i, te, tv, hn, ne, par, nu: (i, 0)),
        scratch_shapes=[pltpu.VMEM((2, d, f), F32), pltpu.VMEM((2, d, f), F32), pltpu.VMEM((2, f, d), F32),
                        pltpu.VMEM((d, f), BF16), pltpu.VMEM((d, f), BF16), pltpu.VMEM((f, d), BF16),
                        pltpu.SemaphoreType.DMA((2,))],
    )
    return pl.pallas_call(
        _experts_kernel,
        grid_spec=grid_spec,
        out_shape=jax.ShapeDtypeStruct(xs.shape, F32),
        compiler_params=_cparams("arbitrary"),
        name="experts",
    )(tile_expert, tile_valid, has_next, next_expert, tile_parity, n_used,
      xs, wg, bg.reshape(e, 1, f), wu, bu.reshape(e, 1, f), wd, bd.reshape(e, 1, d))


def _combine_kernel(cur_ref, nxt_ref, ys_hbm, x2t_ref, gate_ref, g_ref, b_ref, o_ref, buf_ref, sem,
                    *, alpha, n_steps):
    i = pl.program_id(0)
    tc = gate_ref.shape[0]
    n_copies = tc * TOP_K

    def request(slot_ref, b):
        def start(t, c):
            for k in range(TOP_K):
                pltpu.make_async_copy(_token_tile(ys_hbm, slot_ref[0, 0, t * TOP_K + k]),
                                      _token_tile(buf_ref.at[b, k], t), sem.at[b]).start(priority=k % 2)
            return c
        lax.fori_loop(0, tc, start, 0, unroll=2)

    @pl.when(i == 0)
    def _():
        request(cur_ref, 0)

    @pl.when(i + 1 < n_steps)
    def _():
        request(nxt_ref, (i + 1) % 2)

    b = i % 2
    _wait_tiles(ys_hbm, buf_ref.at[b, 0], sem.at[b], n_copies)
    gates = gate_ref[...]
    y = alpha * _load_token_rows(x2t_ref, tc)
    for k in range(TOP_K):
        y = y + gates[:, k:k + 1] * _load_token_rows(buf_ref.at[b, k], tc)
    o_ref[...] = _layer_norm(y, g_ref[...], b_ref[...])


def _combine(ys, slots, x2t, gates, g3, b3, tc, alpha):
    n = x2t.shape[0] // SUBLANES
    d = SUBLANES * LANES
    nt = n // tc
    const = lambda i: (0, 0)
    slots3 = slots.reshape(nt, 1, tc * TOP_K)
    return pl.pallas_call(
        functools.partial(_combine_kernel, alpha=alpha, n_steps=nt),
        grid=(nt,),
        in_specs=[pl.BlockSpec((1, 1, tc * TOP_K), lambda i: (i, 0, 0), memory_space=pltpu.SMEM),
                  pl.BlockSpec((1, 1, tc * TOP_K), lambda i: (jnp.minimum(i + 1, nt - 1), 0, 0),
                               memory_space=pltpu.SMEM),
                  pl.BlockSpec(memory_space=pl.ANY),
                  pl.BlockSpec((tc * SUBLANES, LANES), lambda i: (i, 0)),
                  pl.BlockSpec((tc, LANES), lambda i: (i, 0)),
                  pl.BlockSpec(g3.shape, const), pl.BlockSpec(b3.shape, const)],
        out_specs=pl.BlockSpec((tc, d), lambda i: (i, 0)),
        out_shape=jax.ShapeDtypeStruct((n, d), F32),
        scratch_shapes=[pltpu.VMEM((2, TOP_K, tc * SUBLANES, LANES), ys.dtype),
                        pltpu.SemaphoreType.DMA((2,))],
        compiler_params=_cparams("arbitrary"),
        name="combine",
    )(slots3, slots3, ys, x2t, gates, g3, b3)


def _rope_tables(seq, a_hd, b_hd):
    def cos_sin(pos, dim):
        inv = ROPE_THETA ** (-jnp.arange(0, dim, 2, dtype=F32) / dim)
        ang = pos.astype(F32)[:, None] * inv[None, :]
        return jnp.cos(ang), jnp.sin(ang)

    t = jnp.arange(seq, dtype=jnp.int32)
    cr, sr = cos_sin(t // GRID_W, a_hd // 2)
    cc, sc = cos_sin(t % GRID_W, a_hd // 2)
    cs, ss = cos_sin(t, b_hd)
    twice = lambda a: jnp.concatenate([a, a], axis=-1)
    cos_a = twice(jnp.concatenate([cr, cr, cc, cc], axis=-1))
    sin_a = twice(jnp.concatenate([-sr, sr, -sc, sc], axis=-1))
    cos_b = twice(jnp.concatenate([cs, cs], axis=-1))
    sin_b = twice(jnp.concatenate([-ss, ss], axis=-1))
    return cos_a, sin_a, cos_b, sin_b


def _pick_tile(n, pref):
    t = min(pref, n)
    while n % t:
        t //= 2
    return t


def kernel(x, mem, w_in, a_q_norm, a_k_norm, b_lambda_q1, b_lambda_k1, b_lambda_q2, b_lambda_k2, b_subln, w_mix_out, ln1_g, ln1_b, w_mem_q, w_mem_kv, w_mem_out, ln2_g, ln2_b, w_router, b_router, w_e_gate, b_e_gate, w_e_up, b_e_up, w_e_down, b_e_down, ln3_g, ln3_b):
    batch, seq, d = x.shape
    depth = w_in.shape[0]
    n = batch * seq
    mem_tokens = mem.shape[1]
    n_experts = w_router.shape[-1]
    a_hd = a_q_norm.shape[-1]
    b_hd = b_lambda_q1.shape[-1]
    alpha = (2.0 * depth) ** 0.25

    a_q, a_kv = A_HEADS * a_hd, A_KV_HEADS * a_hd
    b_qk, b_v = B_HEADS * 2 * b_hd, B_HEADS * 2 * b_hd
    edges = [0, a_q, a_q + a_kv, a_q + 2 * a_kv, a_q + 2 * a_kv + b_qk,
             a_q + 2 * a_kv + 2 * b_qk, a_q + 2 * a_kv + 2 * b_qk + b_v]
    splits = tuple(zip(edges[:-1], edges[1:]))
    assert edges[-1] == w_in.shape[-1]

    tm = _pick_tile(seq, 512)
    tq = _pick_tile(seq, 256)
    tg = 512
    tc = _pick_tile(n, 256)
    n_tiles = (n * TOP_K) // tg + n_experts
    cos_a, sin_a, cos_b, sin_b = _rope_tables(seq, a_hd, b_hd)
    seg = jnp.arange(LANES, dtype=jnp.int32) // a_hd
    bd = jnp.where(seg[:, None] == seg[None, :], 1.0 / a_hd, 0.0).astype(BF16)
    row2 = lambda v: v.reshape(1, -1)
    twice = lambda v: jnp.concatenate([v, v]).reshape(1, -1)

    h = x.reshape(n, d)
    mem2d = mem.reshape(batch * mem_tokens, d)
    for layer in range(depth):
        lambda_init = 0.8 - 0.6 * math.exp(-0.3 * layer)
        qa, ka, va, qb, kb, vb = _in_proj(h, w_in[layer].astype(BF16), splits, tm)
        oa = _attn_a(qa, ka, va, cos_a, sin_a, twice(a_q_norm[layer]), twice(a_k_norm[layer]), bd,
                     batch, seq, tq)
        ob = _attn_b(qb, kb, vb, cos_b, sin_b, row2(b_lambda_q1[layer]), row2(b_lambda_k1[layer]),
                     row2(b_lambda_q2[layer]), row2(b_lambda_k2[layer]), row2(b_subln[layer]),
                     batch, seq, tq, lambda_init)
        kvmem = _mem_kv(mem2d, w_mem_kv[layer].astype(BF16), _pick_tile(batch * mem_tokens, 512))
        x2t, code, gates, counts = _mix_xattn(
            oa, ob, h, w_mix_out[layer].astype(BF16), row2(ln1_g[layer]), row2(ln1_b[layer]),
            w_mem_q[layer].astype(BF16), kvmem, w_mem_out[layer].astype(BF16),
            row2(ln2_g[layer]), row2(ln2_b[layer]), w_router[layer].astype(BF16),
            row2(b_router[layer]), seq, mem_tokens, tm, alpha)
        slots, tile_expert, tile_valid, has_next, next_expert, tile_parity, n_used = _tile_map(
            counts.reshape(-1).astype(jnp.int32), code[:, :TOP_K], tg, n_tiles)
        xs = _dispatch(x2t, slots, n_tiles * tg, tm)
        ys = _experts(xs, tile_expert, tile_valid, has_next, next_expert, tile_parity, n_used,
                      w_e_gate[layer], b_e_gate[layer], w_e_up[layer], b_e_up[layer],
                      w_e_down[layer], b_e_down[layer], tg)
        h = _combine(ys, slots, x2t, gates, row2(ln3_g[layer]), row2(ln3_b[layer]), tc, alpha)
    return h.reshape(batch, seq, d)
```

```python
import functools
import math

import jax
import jax.numpy as jnp
import numpy as np
from jax import lax
from jax.experimental import pallas as pl
from jax.experimental.pallas import tpu as pltpu

F32 = jnp.float32
BF16 = jnp.bfloat16

GRID_W = 64
ROPE_THETA = 10000.0
A_HEADS = 8
A_KV_HEADS = 2
A_GROUP = A_HEADS // A_KV_HEADS
B_HEADS = 4
MEM_HEADS = 4
TOP_K = 4
SWIGLU_LIMIT = 7.0
SWIGLU_ALPHA = 1.702
LN_EPS = 1e-5
RMS_EPS = 1e-6
LOG2E = 1.4426950408889634

VMEM_LIMIT_BYTES = 56 * 1024 * 1024
LANES = 128
SUBLANES = 8
CODE_EXPERT_SHIFT = 20
CODE_EXPERT_STRIDE = 1 << CODE_EXPERT_SHIFT
NT_DIMS = (((1,), (1,)), ((), ()))


def _cparams(*sem):
    return pltpu.CompilerParams(dimension_semantics=sem, vmem_limit_bytes=VMEM_LIMIT_BYTES)


def _layer_norm(x, g, b):
    mu = jnp.mean(x, axis=-1, keepdims=True)
    xc = x - mu
    var = jnp.mean(xc * xc, axis=-1, keepdims=True)
    return xc * lax.rsqrt(var + LN_EPS) * g + b


def _rms_norm(x, g):
    return x * lax.rsqrt(jnp.mean(x * x, axis=-1, keepdims=True) + RMS_EPS) * g


def _rope(x, cos, sin_signed, half):
    w = x.shape[-1]
    lane = lax.broadcasted_iota(jnp.int32, (1, w), 1)
    first = (lane & (2 * half - 1)) < half
    partner = jnp.where(first, pltpu.roll(x, w - half, 1), pltpu.roll(x, half, 1))
    return x * cos + partner * sin_signed


def _segment_mean(x, bd):
    hi = x.astype(BF16)
    lo = (x - hi.astype(F32)).astype(BF16)
    return (jnp.dot(hi, bd, preferred_element_type=F32) + jnp.dot(lo, bd, preferred_element_type=F32))


def _swap_lane_halves(x):
    return pltpu.roll(x, x.shape[-1] // 2, 1)


def _in_proj_kernel(x_ref, w_ref, *out_refs, splits):
    xb = x_ref[...].astype(BF16)
    for o_ref, (c0, c1) in zip(out_refs, splits):
        o_ref[...] = jnp.dot(xb, w_ref[:, c0:c1], preferred_element_type=F32).astype(o_ref.dtype)


def _in_proj(x2d, w_bf, splits, tm):
    n, d = x2d.shape
    widths = [c1 - c0 for c0, c1 in splits]
    return pl.pallas_call(
        functools.partial(_in_proj_kernel, splits=splits),
        grid=(n // tm,),
        in_specs=[pl.BlockSpec((tm, d), lambda i: (i, 0)),
                  pl.BlockSpec(w_bf.shape, lambda i: (0, 0))],
        out_specs=[pl.BlockSpec((tm, w), lambda i: (i, 0)) for w in widths],
        out_shape=[jax.ShapeDtypeStruct((n, w), BF16) for w in widths],
        compiler_params=_cparams("parallel"),
        name="in_proj",
    )(x2d, w_bf)


def _attn_a_kernel(q_ref, k_ref, v_ref, cq_ref, sq_ref, ck_ref, sk_ref, gq_ref, gk_ref, bd_ref,
                   o_ref, kn_ref, va_ref, *, hd):
    low = lax.broadcasted_iota(jnp.int32, (1, 2 * hd), 1) < hd

    @pl.when(pl.program_id(1) == 0)
    def _():
        k = k_ref[...].astype(F32)
        k = k * lax.rsqrt(_segment_mean(k * k, bd_ref[...]) + RMS_EPS) * gk_ref[...]
        k = _rope(k, ck_ref[...], sk_ref[...], hd // 4)
        k_sw = _swap_lane_halves(k)
        v = v_ref[...].astype(F32)
        kn_ref[0] = jnp.where(low, k, k_sw).astype(BF16)
        kn_ref[1] = jnp.where(low, k_sw, k).astype(BF16)
        va_ref[0] = jnp.where(low, v, 1.0).astype(BF16)
        va_ref[1] = jnp.where(low, _swap_lane_halves(v), 1.0).astype(BF16)

    scale = hd ** -0.5 * LOG2E

    n_pairs = A_HEADS // 2
    qs = []
    for pair in range(n_pairs):
        q = q_ref[:, pair * 2 * hd:(pair + 1) * 2 * hd].astype(F32)
        q = q * lax.rsqrt(_segment_mean(q * q, bd_ref[...]) + RMS_EPS) * gq_ref[...]
        q = _rope(q, cq_ref[...], sq_ref[...], hd // 4) * scale
        qs.append([jnp.where(keep, q, 0.0).astype(BF16) for keep in (low, jnp.logical_not(low))])

    def scores(pair):
        kv = (2 * pair) // A_GROUP
        return [lax.dot_general(qh, kn_ref[kv], NT_DIMS, preferred_element_type=F32) for qh in qs[pair]]

    outs = []
    s_next = scores(0)
    for pair in range(n_pairs):
        s_cur = s_next
        if pair + 1 < n_pairs:
            s_next = scores(pair + 1)
        kv = (2 * pair) // A_GROUP
        o_pair = []
        for s in s_cur:
            e = jnp.exp2(s - jnp.max(s, axis=-1, keepdims=True)).astype(BF16)
            oa = jnp.dot(e, va_ref[kv], preferred_element_type=F32)
            l = jnp.where(low, _swap_lane_halves(oa), oa)
            o_pair.append(oa * (1.0 / l))
        outs.append(jnp.where(low, o_pair[0], _swap_lane_halves(o_pair[1])))
    o_ref[...] = jnp.concatenate(outs, axis=-1).astype(o_ref.dtype)


def _attn_a(qa, ka, va, cos, sin, gq, gk, bd, batch, seq, tq):
    n, dq = qa.shape
    hd = dq // A_HEADS
    assert 2 * hd == LANES and ka.shape[1] == LANES and va.shape[1] == LANES
    nq = seq // tq
    row = lambda b, i: (b * nq + i, 0)
    per_b = lambda b, i: (b, 0)
    const = lambda b, i: (0, 0)
    return pl.pallas_call(
        functools.partial(_attn_a_kernel, hd=hd),
        grid=(batch, nq),
        in_specs=[pl.BlockSpec((tq, dq), row),
                  pl.BlockSpec((seq, LANES), per_b),
                  pl.BlockSpec((seq, LANES), per_b),
                  pl.BlockSpec((tq, LANES), lambda b, i: (i, 0)),
                  pl.BlockSpec((tq, LANES), lambda b, i: (i, 0)),
                  pl.BlockSpec((seq, LANES), const),
                  pl.BlockSpec((seq, LANES), const),
                  pl.BlockSpec((1, LANES), const),
                  pl.BlockSpec((1, LANES), const),
                  pl.BlockSpec((LANES, LANES), const)],
        out_specs=pl.BlockSpec((tq, dq), row),
        out_shape=jax.ShapeDtypeStruct((n, dq), BF16),
        scratch_shapes=[pltpu.VMEM((A_KV_HEADS, seq, LANES), BF16),
                        pltpu.VMEM((A_KV_HEADS, seq, LANES), BF16)],
        compiler_params=_cparams("arbitrary", "arbitrary"),
        name="attn_a",
    )(qa, ka, va, cos, sin, cos, sin, gq, gk, bd)


def _attn_b_kernel(q_ref, k_ref, v_ref, cq_ref, sq_ref, ck_ref, sk_ref,
                   lq1_ref, lk1_ref, lq2_ref, lk2_ref, g_ref, o_ref, kr_ref, va_ref, *, hd, lambda_init):
    low = lax.broadcasted_iota(jnp.int32, (1, 2 * hd), 1) < hd
    vd = 2 * hd

    @pl.when(pl.program_id(1) == 0)
    def _():
        ones = jnp.ones((v_ref.shape[0], vd), BF16)
        for h in range(B_HEADS):
            k = _rope(k_ref[:, h * 2 * hd:(h + 1) * 2 * hd].astype(F32), ck_ref[...], sk_ref[...], hd // 2)
            k_sw = _swap_lane_halves(k)
            kr_ref[2 * h] = jnp.where(low, k, k_sw).astype(BF16)
            kr_ref[2 * h + 1] = jnp.where(low, k_sw, k).astype(BF16)
            va_ref[h] = jnp.concatenate([v_ref[:, h * vd:(h + 1) * vd], ones], axis=-1)

    lam = (jnp.exp(jnp.sum(lq1_ref[...] * lk1_ref[...], axis=-1, keepdims=True))
           - jnp.exp(jnp.sum(lq2_ref[...] * lk2_ref[...], axis=-1, keepdims=True))
           + lambda_init)
    scale = hd ** -0.5 * LOG2E
    qs = []
    for h in range(B_HEADS):
        q = q_ref[:, h * 2 * hd:(h + 1) * 2 * hd].astype(F32)
        q = _rope(q, cq_ref[...], sq_ref[...], hd // 2) * scale
        qs.append([jnp.where(keep, q, 0.0).astype(BF16) for keep in (low, jnp.logical_not(low))])

    def scores(h):
        return [lax.dot_general(qm, kr_ref[2 * h + c], NT_DIMS, preferred_element_type=F32)
                for c, qm in enumerate(qs[h])]

    outs = []
    s_next = scores(0)
    for h in range(B_HEADS):
        s_cur = s_next
        if h + 1 < B_HEADS:
            s_next = scores(h + 1)
        oa = [jnp.dot(jnp.exp2(s - jnp.max(s, axis=-1, keepdims=True)).astype(BF16), va_ref[h],
                      preferred_element_type=F32) for s in s_cur]
        o = oa[0][:, :vd] * (1.0 / oa[0][:, vd:]) - oa[1][:, :vd] * (lam / oa[1][:, vd:])
        outs.append(_rms_norm(o, g_ref[...]) * (1.0 - lambda_init))
    o_ref[...] = jnp.concatenate(outs, axis=-1).astype(o_ref.dtype)


def _attn_b(qb, kb, vb, cos, sin, lq1, lk1, lq2, lk2, subln, batch, seq, tq, lambda_init):
    n, dq = qb.shape
    hd = dq // (2 * B_HEADS)
    assert 2 * hd == LANES and vb.shape[1] == B_HEADS * LANES
    nq = seq // tq
    row = lambda b, i: (b * nq + i, 0)
    per_b = lambda b, i: (b, 0)
    const = lambda b, i: (0, 0)
    vec = pl.BlockSpec((1, hd), const)
    return pl.pallas_call(
        functools.partial(_attn_b_kernel, hd=hd, lambda_init=lambda_init),
        grid=(batch, nq),
        in_specs=[pl.BlockSpec((tq, dq), row),
                  pl.BlockSpec((seq, dq), per_b),
                  pl.BlockSpec((seq, vb.shape[1]), per_b),
                  pl.BlockSpec((tq, LANES), lambda b, i: (i, 0)),
                  pl.BlockSpec((tq, LANES), lambda b, i: (i, 0)),
                  pl.BlockSpec((seq, LANES), const),
                  pl.BlockSpec((seq, LANES), const),
                  vec, vec, vec, vec,
                  pl.BlockSpec((1, LANES), const)],
        out_specs=pl.BlockSpec((tq, vb.shape[1]), row),
        out_shape=jax.ShapeDtypeStruct((n, vb.shape[1]), BF16),
        scratch_shapes=[pltpu.VMEM((2 * B_HEADS, seq, LANES), BF16),
                        pltpu.VMEM((B_HEADS, seq, 2 * LANES), BF16)],
        compiler_params=_cparams("arbitrary", "arbitrary"),
        name="attn_b",
    )(qb, kb, vb, cos, sin, cos, sin, lq1, lk1, lq2, lk2, subln)


def _matmul_kernel(x_ref, w_ref, o_ref):
    o_ref[...] = jnp.dot(x_ref[...].astype(BF16), w_ref[...],
                         preferred_element_type=F32).astype(o_ref.dtype)


def _mem_kv(mem2d, w_bf, tm):
    n, d = mem2d.shape
    return pl.pallas_call(
        _matmul_kernel,
        grid=(n // tm,),
        in_specs=[pl.BlockSpec((tm, d), lambda i: (i, 0)),
                  pl.BlockSpec(w_bf.shape, lambda i: (0, 0))],
        out_specs=pl.BlockSpec((tm, w_bf.shape[1]), lambda i: (i, 0)),
        out_shape=jax.ShapeDtypeStruct((n, w_bf.shape[1]), BF16),
        compiler_params=_cparams("parallel"),
        name="mem_kv",
    )(mem2d, w_bf)


def _mix_xattn_kernel(oa_ref, ob_ref, x_ref, wmix_ref, g1_ref, b1_ref, wq_ref, kv_ref, wo_ref,
                      g2_ref, b2_ref, wr_ref, br_ref,
                      x2t_ref, code_ref, gate_ref, cnt_ref, carry_ref, *, alpha, n_experts, sub):
    @pl.when(pl.program_id(0) == 0)
    def _():
        carry_ref[...] = jnp.zeros(carry_ref.shape, carry_ref.dtype)

    da = oa_ref.shape[1]
    d = x_ref.shape[1]
    hd = d // MEM_HEADS
    scale = hd ** -0.5
    starts = range(0, x_ref.shape[0], sub)
    subs = [pl.ds(r0, sub) for r0 in starts]

    mix = [jnp.dot(oa_ref[rows, :], wmix_ref[:da, :], preferred_element_type=F32)
           + jnp.dot(ob_ref[rows, :], wmix_ref[da:, :], preferred_element_type=F32) for rows in subs]
    x1 = [_layer_norm(alpha * x_ref[rows, :] + m, g1_ref[...], b1_ref[...]) for rows, m in zip(subs, mix)]
    q = [jnp.dot(v.astype(BF16), wq_ref[...], preferred_element_type=F32).astype(BF16) for v in x1]

    def mem_attention(qv):
        outs = []
        for h in range(MEM_HEADS):
            s = lax.dot_general(qv[:, h * hd:(h + 1) * hd], kv_ref[:, h * hd:(h + 1) * hd],
                                NT_DIMS, preferred_element_type=F32) * scale
            e = jnp.exp(s - jnp.max(s, axis=-1, keepdims=True))
            l = jnp.sum(e, axis=-1, keepdims=True)
            o = jnp.dot(e.astype(BF16), kv_ref[:, d + h * hd:d + (h + 1) * hd],
                        preferred_element_type=F32)
            outs.append((o * (1.0 / l)).astype(BF16))
        return jnp.concatenate(outs, axis=-1)

    att = [mem_attention(qv) for qv in q]
    xa = [jnp.dot(a, wo_ref[...], preferred_element_type=F32) for a in att]
    x2 = [_layer_norm(alpha * v + a, g2_ref[...], b2_ref[...]) for v, a in zip(x1, xa)]
    for r0, v in zip(starts, x2):
        for j in range(SUBLANES):
            x2t_ref[pl.ds(r0 * SUBLANES + j, sub, stride=SUBLANES), :] = v[:, j * LANES:(j + 1) * LANES]
    all_logits = [jnp.dot(v.astype(BF16), wr_ref[...], preferred_element_type=F32) + br_ref[...] for v in x2]

    col = lax.broadcasted_iota(jnp.int32, (sub, n_experts), 1).astype(F32)
    lane = lax.broadcasted_iota(jnp.int32, (sub, LANES), 1)
    r_id = lax.broadcasted_iota(jnp.int32, (sub, sub), 0)
    c_id = lax.broadcasted_iota(jnp.int32, (sub, sub), 1)
    lower = jnp.where(c_id < r_id, 1.0, 0.0).astype(BF16)
    for rows, logits in zip(subs, all_logits):
        vals, idxs = [], []
        for _ in range(TOP_K):
            m = jnp.max(logits, axis=-1, keepdims=True)
            idx = jnp.min(jnp.where(logits == m, col, float(n_experts)), axis=-1, keepdims=True)
            vals.append(m)
            idxs.append(idx)
            logits = jnp.where(col == idx, -jnp.inf, logits)
        es = [jnp.exp(v - vals[0]) for v in vals]
        den = es[0] + es[1] + es[2] + es[3]

        chosen = [col == idx for idx in idxs]
        cnt = sum(c.astype(F32) for c in chosen)
        before = carry_ref[...] + jnp.dot(lower, cnt.astype(BF16), preferred_element_type=F32)
        carry_ref[...] = carry_ref[...] + jnp.sum(cnt, axis=0, keepdims=True)

        code_out = jnp.zeros((sub, LANES), jnp.int32)
        gate_out = jnp.zeros((sub, LANES), F32)
        for k in range(TOP_K):
            rank = jnp.sum(jnp.where(chosen[k], before, 0.0), axis=-1, keepdims=True)
            code = idxs[k].astype(jnp.int32) * CODE_EXPERT_STRIDE + rank.astype(jnp.int32)
            code_out = jnp.where(lane == k, code, code_out)
            gate_out = jnp.where(lane == k, es[k] / den, gate_out)
        code_ref[:, rows] = code_out.T[:SUBLANES, :]
        gate_ref[rows, :] = gate_out
    cnt_ref[...] = carry_ref[...]


def _mix_xattn(oa, ob, x2d, wmix, g1, b1, wq, kvmem, wo, g2, b2, wr, br, seq, mem_tokens, tm, alpha):
    n, d = x2d.shape
    tiles_per_b = seq // tm
    row = lambda i: (i, 0)
    const = lambda i: (0, 0)
    full = lambda a: pl.BlockSpec(a.shape, const)
    n_experts = wr.shape[1]
    assert d == SUBLANES * LANES and n < CODE_EXPERT_STRIDE
    return pl.pallas_call(
        functools.partial(_mix_xattn_kernel, alpha=alpha, n_experts=n_experts, sub=_pick_tile(tm, 256)),
        grid=(n // tm,),
        in_specs=[pl.BlockSpec((tm, oa.shape[1]), row),
                  pl.BlockSpec((tm, ob.shape[1]), row),
                  pl.BlockSpec((tm, d), row),
                  full(wmix), full(g1), full(b1), full(wq),
                  pl.BlockSpec((mem_tokens, kvmem.shape[1]), lambda i: (i // tiles_per_b, 0)),
                  full(wo), full(g2), full(b2), full(wr), full(br)],
        out_specs=[pl.BlockSpec((tm * SUBLANES, LANES), row),
                   pl.BlockSpec((SUBLANES, tm), lambda i: (0, i)), pl.BlockSpec((tm, LANES), row),
                   pl.BlockSpec((1, n_experts), const)],
        out_shape=[jax.ShapeDtypeStruct((n * SUBLANES, LANES), F32),
                   jax.ShapeDtypeStruct((SUBLANES, n), jnp.int32),
                   jax.ShapeDtypeStruct((n, LANES), F32),
                   jax.ShapeDtypeStruct((1, n_experts), F32)],
        scratch_shapes=[pltpu.VMEM((1, n_experts), F32)],
        compiler_params=_cparams("arbitrary"),
        name="mix_xattn",
    )(oa, ob, x2d, wmix, g1, b1, wq, kvmem, wo, g2, b2, wr, br)


def _tile_map(counts, code4, tg, n_tiles):
    n_experts = counts.shape[0]
    experts = jnp.arange(n_experts, dtype=jnp.int32)
    padded = ((counts + tg - 1) // tg) * tg
    pad_end = jnp.cumsum(padded)
    pad_start = pad_end - padded
    n_used = pad_end[-1] // tg

    pair_expert = lax.shift_right_logical(code4, CODE_EXPERT_SHIFT)
    pair_start = jnp.sum(jnp.where(pair_expert[..., None] == experts, pad_start, 0), axis=-1)
    slots = pair_start + (code4 & (CODE_EXPERT_STRIDE - 1))

    tile_id = jnp.arange(n_tiles, dtype=jnp.int32)
    tile_start = tile_id * tg
    tile_expert = jnp.minimum(
        jnp.sum((tile_start[:, None] >= pad_end[None, :]).astype(jnp.int32), axis=1), n_experts - 1)
    onehot = (tile_expert[:, None] == experts[None, :]).astype(jnp.int32)
    pick = lambda per_expert: jnp.sum(onehot * per_expert[None, :], axis=1)
    tile_valid = jnp.clip(pick(pad_start + counts) - tile_start, 0, tg)
    next_tile = pick(pad_end) // tg
    has_next = (next_tile < n_used).astype(jnp.int32)
    next_expert = jnp.sum((tile_id[:, None] == jnp.minimum(next_tile, n_tiles - 1)[None, :]).astype(jnp.int32)
                          * tile_expert[:, None], axis=0)
    nonempty_before = jnp.cumsum((counts > 0).astype(jnp.int32)) - (counts > 0).astype(jnp.int32)
    tile_parity = pick(nonempty_before) & 1
    i32 = lambda a: a.astype(jnp.int32)
    return (i32(slots), i32(tile_expert), i32(tile_valid), i32(has_next), i32(next_expert),
            i32(tile_parity), i32(n_used).reshape(1))


def _token_tile(ref, index):
    if isinstance(index, int):
        return ref.at[pl.ds(index * SUBLANES, SUBLANES)]
    return ref.at[pl.ds(pl.multiple_of(index * SUBLANES, SUBLANES), SUBLANES)]


def _wait_tiles(src_ref, dst_ref, sem, count):
    def wait(_, c):
        pltpu.make_async_copy(_token_tile(src_ref, 0), _token_tile(dst_ref, 0), sem).wait()
        return c
    lax.fori_loop(0, count, wait, 0, unroll=8)


def _load_token_rows(ref, rows):
    return jnp.concatenate([ref[pl.ds(j, rows, stride=SUBLANES), :] for j in range(SUBLANES)], axis=-1)


def _dispatch_kernel(slot_ref, x_ref, o_hbm, sem, *, tokens):
    def start(t, c):
        for k in range(TOP_K):
            pltpu.make_async_copy(_token_tile(x_ref, t), _token_tile(o_hbm, slot_ref[k, t]),
                                  sem).start(priority=k % 2)
        return c
    lax.fori_loop(0, tokens, start, 0, unroll=2)
    _wait_tiles(x_ref, o_hbm, sem, tokens * TOP_K)


def _dispatch(x2t, slots, n_slots, tm):
    n = x2t.shape[0] // SUBLANES
    nt = n // tm
    return pl.pallas_call(
        functools.partial(_dispatch_kernel, tokens=tm),
        grid=(nt,),
        in_specs=[pl.BlockSpec((SUBLANES, tm), lambda i: (0, i), memory_space=pltpu.SMEM),
                  pl.BlockSpec((tm * SUBLANES, LANES), lambda i: (i, 0))],
        out_specs=pl.BlockSpec(memory_space=pl.ANY),
        out_shape=jax.ShapeDtypeStruct((n_slots * SUBLANES, LANES), x2t.dtype),
        scratch_shapes=[pltpu.SemaphoreType.DMA(())],
        compiler_params=_cparams("arbitrary"),
        name="dispatch",
    )(slots, x2t)


def _experts_kernel(te_ref, tv_ref, hn_ref, ne_ref, par_ref, nused_ref,
                    x_ref, wg_hbm, bg_ref, wu_hbm, bu_ref, wd_hbm, bd_ref,
                    o_ref, wg_st, wu_st, wd_st, wgb_ref, wub_ref, wdb_ref, sem):
    i = pl.program_id(0)
    used = i < nused_ref[0]
    first = used & ((i == 0) | (te_ref[i] != te_ref[jnp.maximum(i - 1, 0)]))
    tg = x_ref.shape[0] // SUBLANES

    def weight_copies(expert, slot):
        return [pltpu.make_async_copy(hbm.at[expert], st.at[slot], sem.at[slot])
                for hbm, st in ((wg_hbm, wg_st), (wu_hbm, wu_st), (wd_hbm, wd_st))]

    @pl.when(used & (i == 0))
    def _():
        for c in weight_copies(te_ref[0], par_ref[0]):
            c.start()

    @pl.when(first)
    def _():
        slot = par_ref[i]
        for c in weight_copies(te_ref[i], slot):
            c.wait()

        @pl.when(hn_ref[i] == 1)
        def _():
            for c in weight_copies(ne_ref[i], 1 - slot):
                c.start()

        wgb_ref[...] = wg_st[slot].astype(BF16)
        wub_ref[...] = wu_st[slot].astype(BF16)
        wdb_ref[...] = wd_st[slot].astype(BF16)

    @pl.when(used)
    def _():
        row = lax.broadcasted_iota(jnp.int32, (tg, 1), 0)
        x = jnp.where(row < tv_ref[i], _load_token_rows(x_ref, tg), 0.0).astype(BF16)
        g = jnp.dot(x, wgb_ref[...], preferred_element_type=F32) + bg_ref[0]
        u = jnp.dot(x, wub_ref[...], preferred_element_type=F32) + bu_ref[0]
        g = jnp.minimum(g, SWIGLU_LIMIT)
        u = jnp.clip(u, -SWIGLU_LIMIT, SWIGLU_LIMIT)
        h = g * (1.0 / (1.0 + jnp.exp(-SWIGLU_ALPHA * g))) * (u + 1.0)
        y = jnp.dot(h.astype(BF16), wdb_ref[...], preferred_element_type=F32) + bd_ref[0]
        for j in range(SUBLANES):
            o_ref[pl.ds(j, tg, stride=SUBLANES), :] = y[:, j * LANES:(j + 1) * LANES]

    @pl.when(jnp.logical_not(used))
    def _():
        o_ref[...] = jnp.zeros(o_ref.shape, o_ref.dtype)


def _experts(xs, tile_expert, tile_valid, has_next, next_expert, tile_parity, n_used,
             wg, bg, wu, bu, wd, bd, tg):
    e, d, f = wg.shape
    n_tiles = xs.shape[0] // (tg * SUBLANES)
    xrow = lambda i, te, tv, hn, ne, par, nu: (jnp.minimum(i, nu[0] - 1), 0)
    bmap = lambda i, te, tv, hn, ne, par, nu: (te[i], 0, 0)
    hbm = pl.BlockSpec(memory_space=pl.ANY)
    grid_spec = pltpu.PrefetchScalarGridSpec(
        num_scalar_prefetch=6,
        grid=(n_tiles,),
        in_specs=[pl.BlockSpec((tg * SUBLANES, LANES), xrow),
                  hbm, pl.BlockSpec((1, 1, f), bmap),
                  hbm, pl.BlockSpec((1, 1, f), bmap),
                  hbm, pl.BlockSpec((1, 1, d), bmap)],
        out_specs=pl.BlockSpec((tg * SUBLANES, LANES), lambda i, te, tv, hn, ne, par, nu: (i, 0)),
        scratch_shapes=[pltpu.VMEM((2, d, f), F32), pltpu.VMEM((2, d, f), F32), pltpu.VMEM((2, f, d), F32),
                        pltpu.VMEM((d, f), BF16), pltpu.VMEM((d, f), BF16), pltpu.VMEM((f, d), BF16),
                        pltpu.SemaphoreType.DMA((2,))],
    )
    return pl.pallas_call(
        _experts_kernel,
        grid_spec=grid_spec,
        out_shape=jax.ShapeDtypeStruct(xs.shape, F32),
        compiler_params=_cparams("arbitrary"),
        name="experts",
    )(tile_expert, tile_valid, has_next, next_expert, tile_parity, n_used,
      xs, wg, bg.reshape(e, 1, f), wu, bu.reshape(e, 1, f), wd, bd.reshape(e, 1, d))


def _combine_kernel(cur_ref, nxt_ref, ys_hbm, x2t_ref, gate_ref, g_ref, b_ref, o_ref, buf0_ref, buf1_ref, sem,
                    *, alpha, n_steps):
    i = pl.program_id(0)
    tc = gate_ref.shape[0]
    n_copies = tc * TOP_K
    bufs = (buf0_ref, buf1_ref)

    def copy(slot_ref, b, t, k):
        return pltpu.make_async_copy(_token_tile(ys_hbm, slot_ref[k, t]),
                                     _token_tile(bufs[b].at[k], t), sem.at[b])

    @pl.when(i == 0)
    def _():
        def start(t, c):
            for k in range(TOP_K):
                copy(cur_ref, 0, t, k).start(priority=k % 2)
            return c
        lax.fori_loop(0, tc, start, 0, unroll=2)

    for b in range(2):
        @pl.when(i % 2 == b)
        def _(b=b):
            _wait_tiles(ys_hbm, bufs[b].at[0], sem.at[b], n_copies)
            for t in range(tc):
                for k in range(TOP_K):
                    copy(nxt_ref, 1 - b, t, k).start(priority=k % 2)
            gates = gate_ref[...]
            y = alpha * _load_token_rows(x2t_ref, tc)
            for k in range(TOP_K):
                y = y + gates[:, k:k + 1] * _load_token_rows(bufs[b].at[k], tc)
            o_ref[...] = _layer_norm(y, g_ref[...], b_ref[...])

            @pl.when(i == n_steps - 1)
            def _():
                _wait_tiles(ys_hbm, bufs[1 - b].at[0], sem.at[1 - b], n_copies)


def _combine(ys, slots, x2t, gates, g3, b3, tc, alpha):
    n = x2t.shape[0] // SUBLANES
    d = SUBLANES * LANES
    nt = n // tc
    const = lambda i: (0, 0)
    return pl.pallas_call(
        functools.partial(_combine_kernel, alpha=alpha, n_steps=nt),
        grid=(nt,),
        in_specs=[pl.BlockSpec((SUBLANES, tc), lambda i: (0, i), memory_space=pltpu.SMEM),
                  pl.BlockSpec((SUBLANES, tc), lambda i: (0, jnp.minimum(i + 1, nt - 1)),
                               memory_space=pltpu.SMEM),
                  pl.BlockSpec(memory_space=pl.ANY),
                  pl.BlockSpec((tc * SUBLANES, LANES), lambda i: (i, 0)),
                  pl.BlockSpec((tc, LANES), lambda i: (i, 0)),
                  pl.BlockSpec(g3.shape, const), pl.BlockSpec(b3.shape, const)],
        out_specs=pl.BlockSpec((tc, d), lambda i: (i, 0)),
        out_shape=jax.ShapeDtypeStruct((n, d), F32),
        scratch_shapes=[pltpu.VMEM((TOP_K, tc * SUBLANES, LANES), ys.dtype),
                        pltpu.VMEM((TOP_K, tc * SUBLANES, LANES), ys.dtype),
                        pltpu.SemaphoreType.DMA((2,))],
        compiler_params=_cparams("arbitrary"),
        name="combine",
    )(slots, slots, ys, x2t, gates, g3, b3)


def _rope_tables(seq, a_hd, b_hd):
    def cos_sin(pos, dim):
        inv = ROPE_THETA ** (-np.arange(0, dim, 2, dtype=np.float64) / dim)
        ang = pos.astype(np.float64)[:, None] * inv[None, :]
        return np.cos(ang).astype(np.float32), np.sin(ang).astype(np.float32)

    t = np.arange(seq, dtype=np.int32)
    cr, sr = cos_sin(t // GRID_W, a_hd // 2)
    cc, sc = cos_sin(t % GRID_W, a_hd // 2)
    cs, ss = cos_sin(t, b_hd)
    twice = lambda a: np.concatenate([a, a], axis=-1)
    cos_a = twice(np.concatenate([cr, cr, cc, cc], axis=-1))
    sin_a = twice(np.concatenate([-sr, sr, -sc, sc], axis=-1))
    cos_b = twice(np.concatenate([cs, cs], axis=-1))
    sin_b = twice(np.concatenate([-ss, ss], axis=-1))
    return tuple(jnp.asarray(a) for a in (cos_a, sin_a, cos_b, sin_b))


def _pick_tile(n, pref):
    t = min(pref, n)
    while n % t:
        t //= 2
    return t


def kernel(x, mem, w_in, a_q_norm, a_k_norm, b_lambda_q1, b_lambda_k1, b_lambda_q2, b_lambda_k2, b_subln, w_mix_out, ln1_g, ln1_b, w_mem_q, w_mem_kv, w_mem_out, ln2_g, ln2_b, w_router, b_router, w_e_gate, b_e_gate, w_e_up, b_e_up, w_e_down, b_e_down, ln3_g, ln3_b):
    batch, seq, d = x.shape
    depth = w_in.shape[0]
    n = batch * seq
    mem_tokens = mem.shape[1]
    n_experts = w_router.shape[-1]
    a_hd = a_q_norm.shape[-1]
    b_hd = b_lambda_q1.shape[-1]
    alpha = (2.0 * depth) ** 0.25

    a_q, a_kv = A_HEADS * a_hd, A_KV_HEADS * a_hd
    b_qk, b_v = B_HEADS * 2 * b_hd, B_HEADS * 2 * b_hd
    edges = [0, a_q, a_q + a_kv, a_q + 2 * a_kv, a_q + 2 * a_kv + b_qk,
             a_q + 2 * a_kv + 2 * b_qk, a_q + 2 * a_kv + 2 * b_qk + b_v]
    splits = tuple(zip(edges[:-1], edges[1:]))
    assert edges[-1] == w_in.shape[-1]

    tm = _pick_tile(seq, 512)
    tq = _pick_tile(seq, 512)
    tg = 512
    tc = _pick_tile(n, 256)
    n_tiles = (n * TOP_K) // tg + n_experts
    cos_a, sin_a, cos_b, sin_b = _rope_tables(seq, a_hd, b_hd)
    seg = jnp.arange(LANES, dtype=jnp.int32) // a_hd
    bd = jnp.where(seg[:, None] == seg[None, :], 1.0 / a_hd, 0.0).astype(BF16)
    row2 = lambda v: v.reshape(1, -1)
    twice = lambda v: jnp.concatenate([v, v]).reshape(1, -1)

    h = x.reshape(n, d)
    mem2d = mem.reshape(batch * mem_tokens, d)
    for layer in range(depth):
        lambda_init = 0.8 - 0.6 * math.exp(-0.3 * layer)
        qa, ka, va, qb, kb, vb = _in_proj(h, w_in[layer].astype(BF16), splits, tm)
        oa = _attn_a(qa, ka, va, cos_a, sin_a, twice(a_q_norm[layer]), twice(a_k_norm[layer]), bd,
                     batch, seq, tq)
        ob = _attn_b(qb, kb, vb, cos_b, sin_b, row2(b_lambda_q1[layer]), row2(b_lambda_k1[layer]),
                     row2(b_lambda_q2[layer]), row2(b_lambda_k2[layer]), row2(b_subln[layer]),
                     batch, seq, tq, lambda_init)
        kvmem = _mem_kv(mem2d, w_mem_kv[layer].astype(BF16), _pick_tile(batch * mem_tokens, 512))
        x2t, code, gates, counts = _mix_xattn(
            oa, ob, h, w_mix_out[layer].astype(BF16), row2(ln1_g[layer]), row2(ln1_b[layer]),
            w_mem_q[layer].astype(BF16), kvmem, w_mem_out[layer].astype(BF16),
            row2(ln2_g[layer]), row2(ln2_b[layer]), w_router[layer].astype(BF16),
            row2(b_router[layer]), seq, mem_tokens, tm, alpha)
        slots, tile_expert, tile_valid, has_next, next_expert, tile_parity, n_used = _tile_map(
            counts.reshape(-1).astype(jnp.int32), code, tg, n_tiles)
        xs = _dispatch(x2t, slots, n_tiles * tg, tm)
        ys = _experts(xs, tile_expert, tile_valid, has_next, next_expert, tile_parity, n_used,
                      w_e_gate[layer], b_e_gate[layer], w_e_up[layer], b_e_up[layer],
                      w_e_down[layer], b_e_down[layer], tg)
        h = _combine(ys, slots, x2t, gates, row2(ln3_g[layer]), row2(ln3_b[layer]), tc, alpha)
    return h.reshape(batch, seq, d)
```

```python
import functools
import math

import jax
import jax.numpy as jnp
import numpy as np
from jax import lax
from jax.experimental import pallas as pl
from jax.experimental.pallas import tpu as pltpu

F32 = jnp.float32
BF16 = jnp.bfloat16

GRID_W = 64
ROPE_THETA = 10000.0
A_HEADS = 8
A_KV_HEADS = 2
A_GROUP = A_HEADS // A_KV_HEADS
B_HEADS = 4
MEM_HEADS = 4
TOP_K = 4
SWIGLU_LIMIT = 7.0
SWIGLU_ALPHA = 1.702
LN_EPS = 1e-5
RMS_EPS = 1e-6
LOG2E = 1.4426950408889634

VMEM_LIMIT_BYTES = 56 * 1024 * 1024
LANES = 128
SUBLANES = 8
CODE_EXPERT_SHIFT = 20
CODE_EXPERT_STRIDE = 1 << CODE_EXPERT_SHIFT
NT_DIMS = (((1,), (1,)), ((), ()))


def _cparams(*sem):
    return pltpu.CompilerParams(dimension_semantics=sem, vmem_limit_bytes=VMEM_LIMIT_BYTES)


def _layer_norm(x, g, b):
    mu = jnp.mean(x, axis=-1, keepdims=True)
    xc = x - mu
    var = jnp.mean(xc * xc, axis=-1, keepdims=True)
    return xc * lax.rsqrt(var + LN_EPS) * g + b


def _rms_norm(x, g):
    return x * lax.rsqrt(jnp.mean(x * x, axis=-1, keepdims=True) + RMS_EPS) * g


def _rope(x, cos, sin_signed, half):
    w = x.shape[-1]
    lane = lax.broadcasted_iota(jnp.int32, (1, w), 1)
    first = (lane & (2 * half - 1)) < half
    partner = jnp.where(first, pltpu.roll(x, w - half, 1), pltpu.roll(x, half, 1))
    return x * cos + partner * sin_signed


def _segment_mean(x, bd):
    hi = x.astype(BF16)
    lo = (x - hi.astype(F32)).astype(BF16)
    return (jnp.dot(hi, bd, preferred_element_type=F32) + jnp.dot(lo, bd, preferred_element_type=F32))


def _swap_lane_halves(x):
    return pltpu.roll(x, x.shape[-1] // 2, 1)


def _in_proj_kernel(x_ref, w_ref, *out_refs, splits):
    xb = x_ref[...].astype(BF16)
    for o_ref, (c0, c1) in zip(out_refs, splits):
        o_ref[...] = jnp.dot(xb, w_ref[:, c0:c1], preferred_element_type=F32).astype(o_ref.dtype)


def _in_proj(x2d, w_bf, splits, tm):
    n, d = x2d.shape
    widths = [c1 - c0 for c0, c1 in splits]
    return pl.pallas_call(
        functools.partial(_in_proj_kernel, splits=splits),
        grid=(n // tm,),
        in_specs=[pl.BlockSpec((tm, d), lambda i: (i, 0)),
                  pl.BlockSpec(w_bf.shape, lambda i: (0, 0))],
        out_specs=[pl.BlockSpec((tm, w), lambda i: (i, 0)) for w in widths],
        out_shape=[jax.ShapeDtypeStruct((n, w), BF16) for w in widths],
        compiler_params=_cparams("parallel"),
        name="in_proj",
    )(x2d, w_bf)


def _attn_a_kernel(q_ref, k_ref, v_ref, cq_ref, sq_ref, ck_ref, sk_ref, gq_ref, gk_ref, bd_ref,
                   o_ref, kn_ref, va_ref, *, hd):
    low = lax.broadcasted_iota(jnp.int32, (1, 2 * hd), 1) < hd

    @pl.when(pl.program_id(1) == 0)
    def _():
        k = k_ref[...].astype(F32)
        k = k * lax.rsqrt(_segment_mean(k * k, bd_ref[...]) + RMS_EPS) * gk_ref[...]
        k = _rope(k, ck_ref[...], sk_ref[...], hd // 4)
        k_sw = _swap_lane_halves(k)
        v = v_ref[...].astype(F32)
        kn_ref[0] = jnp.where(low, k, k_sw).astype(BF16)
        kn_ref[1] = jnp.where(low, k_sw, k).astype(BF16)
        va_ref[0] = jnp.where(low, v, 1.0).astype(BF16)
        va_ref[1] = jnp.where(low, _swap_lane_halves(v), 1.0).astype(BF16)

    scale = hd ** -0.5 * LOG2E

    n_pairs = A_HEADS // 2
    qs = []
    for pair in range(n_pairs):
        q = q_ref[:, pair * 2 * hd:(pair + 1) * 2 * hd].astype(F32)
        q = q * lax.rsqrt(_segment_mean(q * q, bd_ref[...]) + RMS_EPS) * gq_ref[...]
        q = _rope(q, cq_ref[...], sq_ref[...], hd // 4) * scale
        qs.append([jnp.where(keep, q, 0.0).astype(BF16) for keep in (low, jnp.logical_not(low))])

    def scores(pair):
        kv = (2 * pair) // A_GROUP
        return [lax.dot_general(qh, kn_ref[kv], NT_DIMS, preferred_element_type=F32) for qh in qs[pair]]

    outs = []
    s_next = scores(0)
    for pair in range(n_pairs):
        s_cur = s_next
        if pair + 1 < n_pairs:
            s_next = scores(pair + 1)
        kv = (2 * pair) // A_GROUP
        o_pair = []
        for s in s_cur:
            e = jnp.exp2(s - jnp.max(s, axis=-1, keepdims=True)).astype(BF16)
            oa = jnp.dot(e, va_ref[kv], preferred_element_type=F32)
            l = jnp.where(low, _swap_lane_halves(oa), oa)
            o_pair.append(oa * (1.0 / l))
        outs.append(jnp.where(low, o_pair[0], _swap_lane_halves(o_pair[1])))
    o_ref[...] = jnp.concatenate(outs, axis=-1).astype(o_ref.dtype)


def _attn_a(qa, ka, va, cos, sin, gq, gk, bd, batch, seq, tq):
    n, dq = qa.shape
    hd = dq // A_HEADS
    assert 2 * hd == LANES and ka.shape[1] == LANES and va.shape[1] == LANES
    nq = seq // tq
    row = lambda b, i: (b * nq + i, 0)
    per_b = lambda b, i: (b, 0)
    const = lambda b, i: (0, 0)
    return pl.pallas_call(
        functools.partial(_attn_a_kernel, hd=hd),
        grid=(batch, nq),
        in_specs=[pl.BlockSpec((tq, dq), row),
                  pl.BlockSpec((seq, LANES), per_b),
                  pl.BlockSpec((seq, LANES), per_b),
                  pl.BlockSpec((tq, LANES), lambda b, i: (i, 0)),
                  pl.BlockSpec((tq, LANES), lambda b, i: (i, 0)),
                  pl.BlockSpec((seq, LANES), const),
                  pl.BlockSpec((seq, LANES), const),
                  pl.BlockSpec((1, LANES), const),
                  pl.BlockSpec((1, LANES), const),
                  pl.BlockSpec((LANES, LANES), const)],
        out_specs=pl.BlockSpec((tq, dq), row),
        out_shape=jax.ShapeDtypeStruct((n, dq), BF16),
        scratch_shapes=[pltpu.VMEM((A_KV_HEADS, seq, LANES), BF16),
                        pltpu.VMEM((A_KV_HEADS, seq, LANES), BF16)],
        compiler_params=_cparams("arbitrary", "arbitrary"),
        name="attn_a",
    )(qa, ka, va, cos, sin, cos, sin, gq, gk, bd)


def _attn_b_kernel(q_ref, k_ref, v_ref, cq_ref, sq_ref, ck_ref, sk_ref,
                   lq1_ref, lk1_ref, lq2_ref, lk2_ref, g_ref, o_ref, kr_ref, va_ref, *, hd, lambda_init):
    low = lax.broadcasted_iota(jnp.int32, (1, 2 * hd), 1) < hd
    vd = 2 * hd

    @pl.when(pl.program_id(1) == 0)
    def _():
        ones = jnp.ones((v_ref.shape[0], vd), BF16)
        for h in range(B_HEADS):
            k = _rope(k_ref[:, h * 2 * hd:(h + 1) * 2 * hd].astype(F32), ck_ref[...], sk_ref[...], hd // 2)
            k_sw = _swap_lane_halves(k)
            kr_ref[2 * h] = jnp.where(low, k, k_sw).astype(BF16)
            kr_ref[2 * h + 1] = jnp.where(low, k_sw, k).astype(BF16)
            va_ref[h] = jnp.concatenate([v_ref[:, h * vd:(h + 1) * vd], ones], axis=-1)

    lam = (jnp.exp(jnp.sum(lq1_ref[...] * lk1_ref[...], axis=-1, keepdims=True))
           - jnp.exp(jnp.sum(lq2_ref[...] * lk2_ref[...], axis=-1, keepdims=True))
           + lambda_init)
    scale = hd ** -0.5 * LOG2E
    qs = []
    for h in range(B_HEADS):
        q = q_ref[:, h * 2 * hd:(h + 1) * 2 * hd].astype(F32)
        q = _rope(q, cq_ref[...], sq_ref[...], hd // 2) * scale
        qs.append([jnp.where(keep, q, 0.0).astype(BF16) for keep in (low, jnp.logical_not(low))])

    def scores(h):
        return [lax.dot_general(qm, kr_ref[2 * h + c], NT_DIMS, preferred_element_type=F32)
                for c, qm in enumerate(qs[h])]

    outs = []
    s_next = scores(0)
    for h in range(B_HEADS):
        s_cur = s_next
        if h + 1 < B_HEADS:
            s_next = scores(h + 1)
        oa = [jnp.dot(jnp.exp2(s - jnp.max(s, axis=-1, keepdims=True)).astype(BF16), va_ref[h],
                      preferred_element_type=F32) for s in s_cur]
        o = oa[0][:, :vd] * (1.0 / oa[0][:, vd:]) - oa[1][:, :vd] * (lam / oa[1][:, vd:])
        outs.append(_rms_norm(o, g_ref[...]) * (1.0 - lambda_init))
    o_ref[...] = jnp.concatenate(outs, axis=-1).astype(o_ref.dtype)


def _attn_b(qb, kb, vb, cos, sin, lq1, lk1, lq2, lk2, subln, batch, seq, tq, lambda_init):
    n, dq = qb.shape
    hd = dq // (2 * B_HEADS)
    assert 2 * hd == LANES and vb.shape[1] == B_HEADS * LANES
    nq = seq // tq
    row = lambda b, i: (b * nq + i, 0)
    per_b = lambda b, i: (b, 0)
    const = lambda b, i: (0, 0)
    vec = pl.BlockSpec((1, hd), const)
    return pl.pallas_call(
        functools.partial(_attn_b_kernel, hd=hd, lambda_init=lambda_init),
        grid=(batch, nq),
        in_specs=[pl.BlockSpec((tq, dq), row),
                  pl.BlockSpec((seq, dq), per_b),
                  pl.BlockSpec((seq, vb.shape[1]), per_b),
                  pl.BlockSpec((tq, LANES), lambda b, i: (i, 0)),
                  pl.BlockSpec((tq, LANES), lambda b, i: (i, 0)),
                  pl.BlockSpec((seq, LANES), const),
                  pl.BlockSpec((seq, LANES), const),
                  vec, vec, vec, vec,
                  pl.BlockSpec((1, LANES), const)],
        out_specs=pl.BlockSpec((tq, vb.shape[1]), row),
        out_shape=jax.ShapeDtypeStruct((n, vb.shape[1]), BF16),
        scratch_shapes=[pltpu.VMEM((2 * B_HEADS, seq, LANES), BF16),
                        pltpu.VMEM((B_HEADS, seq, 2 * LANES), BF16)],
        compiler_params=_cparams("arbitrary", "arbitrary"),
        name="attn_b",
    )(qb, kb, vb, cos, sin, cos, sin, lq1, lk1, lq2, lk2, subln)


def _matmul_kernel(x_ref, w_ref, o_ref):
    o_ref[...] = jnp.dot(x_ref[...].astype(BF16), w_ref[...],
                         preferred_element_type=F32).astype(o_ref.dtype)


def _mem_kv(mem2d, w_bf, tm):
    n, d = mem2d.shape
    return pl.pallas_call(
        _matmul_kernel,
        grid=(n // tm,),
        in_specs=[pl.BlockSpec((tm, d), lambda i: (i, 0)),
                  pl.BlockSpec(w_bf.shape, lambda i: (0, 0))],
        out_specs=pl.BlockSpec((tm, w_bf.shape[1]), lambda i: (i, 0)),
        out_shape=jax.ShapeDtypeStruct((n, w_bf.shape[1]), BF16),
        compiler_params=_cparams("parallel"),
        name="mem_kv",
    )(mem2d, w_bf)


def _mix_xattn_kernel(oa_ref, ob_ref, x_ref, wmix_ref, g1_ref, b1_ref, wq_ref, kv_ref, wo_ref,
                      g2_ref, b2_ref, wr_ref, br_ref,
                      x2t_ref, code_ref, gate_ref, cnt_ref, carry_ref, *, alpha, n_experts, sub):
    @pl.when(pl.program_id(0) == 0)
    def _():
        carry_ref[...] = jnp.zeros(carry_ref.shape, carry_ref.dtype)

    da = oa_ref.shape[1]
    d = x_ref.shape[1]
    hd = d // MEM_HEADS
    scale = hd ** -0.5
    starts = range(0, x_ref.shape[0], sub)
    subs = [pl.ds(r0, sub) for r0 in starts]

    mix = [jnp.dot(oa_ref[rows, :], wmix_ref[:da, :], preferred_element_type=F32)
           + jnp.dot(ob_ref[rows, :], wmix_ref[da:, :], preferred_element_type=F32) for rows in subs]
    x1 = [_layer_norm(alpha * x_ref[rows, :] + m, g1_ref[...], b1_ref[...]) for rows, m in zip(subs, mix)]
    q = [jnp.dot(v.astype(BF16), wq_ref[...], preferred_element_type=F32).astype(BF16) for v in x1]

    def mem_attention(qv):
        outs = []
        for h in range(MEM_HEADS):
            s = lax.dot_general(qv[:, h * hd:(h + 1) * hd], kv_ref[:, h * hd:(h + 1) * hd],
                                NT_DIMS, preferred_element_type=F32) * scale
            e = jnp.exp(s - jnp.max(s, axis=-1, keepdims=True))
            l = jnp.sum(e, axis=-1, keepdims=True)
            o = jnp.dot(e.astype(BF16), kv_ref[:, d + h * hd:d + (h + 1) * hd],
                        preferred_element_type=F32)
            outs.append((o * (1.0 / l)).astype(BF16))
        return jnp.concatenate(outs, axis=-1)

    att = [mem_attention(qv) for qv in q]
    xa = [jnp.dot(a, wo_ref[...], preferred_element_type=F32) for a in att]
    x2 = [_layer_norm(alpha * v + a, g2_ref[...], b2_ref[...]) for v, a in zip(x1, xa)]
    for r0, v in zip(starts, x2):
        for j in range(SUBLANES):
            x2t_ref[pl.ds(r0 * SUBLANES + j, sub, stride=SUBLANES), :] = v[:, j * LANES:(j + 1) * LANES]
    all_logits = [jnp.dot(v.astype(BF16), wr_ref[...], preferred_element_type=F32) + br_ref[...] for v in x2]

    col = lax.broadcasted_iota(jnp.int32, (sub, n_experts), 1).astype(F32)
    lane = lax.broadcasted_iota(jnp.int32, (sub, LANES), 1)
    r_id = lax.broadcasted_iota(jnp.int32, (sub, sub), 0)
    c_id = lax.broadcasted_iota(jnp.int32, (sub, sub), 1)
    lower = jnp.where(c_id < r_id, 1.0, 0.0).astype(BF16)
    for rows, logits in zip(subs, all_logits):
        vals, idxs = [], []
        for _ in range(TOP_K):
            m = jnp.max(logits, axis=-1, keepdims=True)
            idx = jnp.min(jnp.where(logits == m, col, float(n_experts)), axis=-1, keepdims=True)
            vals.append(m)
            idxs.append(idx)
            logits = jnp.where(col == idx, -jnp.inf, logits)
        es = [jnp.exp(v - vals[0]) for v in vals]
        den = es[0] + es[1] + es[2] + es[3]

        chosen = [col == idx for idx in idxs]
        cnt = sum(c.astype(F32) for c in chosen)
        before = carry_ref[...] + jnp.dot(lower, cnt.astype(BF16), preferred_element_type=F32)
        carry_ref[...] = carry_ref[...] + jnp.sum(cnt, axis=0, keepdims=True)

        code_out = jnp.zeros((sub, LANES), jnp.int32)
        gate_out = jnp.zeros((sub, LANES), F32)
        for k in range(TOP_K):
            rank = jnp.sum(jnp.where(chosen[k], before, 0.0), axis=-1, keepdims=True)
            code = idxs[k].astype(jnp.int32) * CODE_EXPERT_STRIDE + rank.astype(jnp.int32)
            code_out = jnp.where(lane == k, code, code_out)
            gate_out = jnp.where(lane == k, es[k] / den, gate_out)
        code_ref[:, rows] = code_out.T[:SUBLANES, :]
        gate_ref[rows, :] = gate_out
    cnt_ref[...] = carry_ref[...]


def _mix_xattn(oa, ob, x2d, wmix, g1, b1, wq, kvmem, wo, g2, b2, wr, br, seq, mem_tokens, tm, alpha):
    n, d = x2d.shape
    tiles_per_b = seq // tm
    row = lambda i: (i, 0)
    const = lambda i: (0, 0)
    full = lambda a: pl.BlockSpec(a.shape, const)
    n_experts = wr.shape[1]
    assert d == SUBLANES * LANES and n < CODE_EXPERT_STRIDE
    return pl.pallas_call(
        functools.partial(_mix_xattn_kernel, alpha=alpha, n_experts=n_experts, sub=_pick_tile(tm, 256)),
        grid=(n // tm,),
        in_specs=[pl.BlockSpec((tm, oa.shape[1]), row),
                  pl.BlockSpec((tm, ob.shape[1]), row),
                  pl.BlockSpec((tm, d), row),
                  full(wmix), full(g1), full(b1), full(wq),
                  pl.BlockSpec((mem_tokens, kvmem.shape[1]), lambda i: (i // tiles_per_b, 0)),
                  full(wo), full(g2), full(b2), full(wr), full(br)],
        out_specs=[pl.BlockSpec((tm * SUBLANES, LANES), row),
                   pl.BlockSpec((SUBLANES, tm), lambda i: (0, i)), pl.BlockSpec((tm, LANES), row),
                   pl.BlockSpec((1, n_experts), const)],
        out_shape=[jax.ShapeDtypeStruct((n * SUBLANES, LANES), F32),
                   jax.ShapeDtypeStruct((SUBLANES, n), jnp.int32),
                   jax.ShapeDtypeStruct((n, LANES), F32),
                   jax.ShapeDtypeStruct((1, n_experts), F32)],
        scratch_shapes=[pltpu.VMEM((1, n_experts), F32)],
        compiler_params=_cparams("arbitrary"),
        name="mix_xattn",
    )(oa, ob, x2d, wmix, g1, b1, wq, kvmem, wo, g2, b2, wr, br)


def _tile_map(counts, code4, tg, n_tiles):
    n_experts = counts.shape[0]
    experts = jnp.arange(n_experts, dtype=jnp.int32)
    padded = ((counts + tg - 1) // tg) * tg
    pad_end = jnp.cumsum(padded)
    pad_start = pad_end - padded
    n_used = pad_end[-1] // tg

    pair_expert = lax.shift_right_logical(code4, CODE_EXPERT_SHIFT)
    pair_start = jnp.sum(jnp.where(pair_expert[..., None] == experts, pad_start, 0), axis=-1)
    slots = pair_start + (code4 & (CODE_EXPERT_STRIDE - 1))

    tile_id = jnp.arange(n_tiles, dtype=jnp.int32)
    tile_start = tile_id * tg
    tile_expert = jnp.minimum(
        jnp.sum((tile_start[:, None] >= pad_end[None, :]).astype(jnp.int32), axis=1), n_experts - 1)
    onehot = (tile_expert[:, None] == experts[None, :]).astype(jnp.int32)
    pick = lambda per_expert: jnp.sum(onehot * per_expert[None, :], axis=1)
    tile_valid = jnp.clip(pick(pad_start + counts) - tile_start, 0, tg)
    next_tile = pick(pad_end) // tg
    has_next = (next_tile < n_used).astype(jnp.int32)
    next_expert = jnp.sum((tile_id[:, None] == jnp.minimum(next_tile, n_tiles - 1)[None, :]).astype(jnp.int32)
                          * tile_expert[:, None], axis=0)
    nonempty_before = jnp.cumsum((counts > 0).astype(jnp.int32)) - (counts > 0).astype(jnp.int32)
    tile_parity = pick(nonempty_before) & 1
    i32 = lambda a: a.astype(jnp.int32)
    return (i32(slots), i32(tile_expert), i32(tile_valid), i32(has_next), i32(next_expert),
            i32(tile_parity), i32(n_used).reshape(1))


def _token_tile(ref, index):
    if isinstance(index, int):
        return ref.at[pl.ds(index * SUBLANES, SUBLANES)]
    return ref.at[pl.ds(pl.multiple_of(index * SUBLANES, SUBLANES), SUBLANES)]


def _wait_tiles(src_ref, dst_ref, sem, count):
    def wait(_, c):
        pltpu.make_async_copy(_token_tile(src_ref, 0), _token_tile(dst_ref, 0), sem).wait()
        return c
    lax.fori_loop(0, count, wait, 0, unroll=8)


def _load_token_rows(ref, rows):
    return jnp.concatenate([ref[pl.ds(j, rows, stride=SUBLANES), :] for j in range(SUBLANES)], axis=-1)


def _dispatch_kernel(slot_ref, x_ref, o_hbm, sem, *, tokens):
    def start(t, c):
        for k in range(TOP_K):
            pltpu.make_async_copy(_token_tile(x_ref, t), _token_tile(o_hbm, slot_ref[k, t]),
                                  sem).start(priority=k % 2)
        return c
    lax.fori_loop(0, tokens, start, 0, unroll=2)
    _wait_tiles(x_ref, o_hbm, sem, tokens * TOP_K)


def _dispatch(x2t, slots, n_slots, tm):
    n = x2t.shape[0] // SUBLANES
    nt = n // tm
    return pl.pallas_call(
        functools.partial(_dispatch_kernel, tokens=tm),
        grid=(nt,),
        in_specs=[pl.BlockSpec((SUBLANES, tm), lambda i: (0, i), memory_space=pltpu.SMEM),
                  pl.BlockSpec((tm * SUBLANES, LANES), lambda i: (i, 0))],
        out_specs=pl.BlockSpec(memory_space=pl.ANY),
        out_shape=jax.ShapeDtypeStruct((n_slots * SUBLANES, LANES), x2t.dtype),
        scratch_shapes=[pltpu.SemaphoreType.DMA(())],
        compiler_params=_cparams("arbitrary"),
        name="dispatch",
    )(slots, x2t)


def _experts_kernel(te_ref, tv_ref, hn_ref, ne_ref, par_ref, nused_ref,
                    x_ref, wg_hbm, bg_ref, wu_hbm, bu_ref, wd_hbm, bd_ref,
                    o_ref, wg_st, wu_st, wd_st, wgb_ref, wub_ref, wdb_ref, sem):
    i = pl.program_id(0)
    used = i < nused_ref[0]
    first = used & ((i == 0) | (te_ref[i] != te_ref[jnp.maximum(i - 1, 0)]))
    tg = x_ref.shape[0] // SUBLANES

    def weight_copies(expert, slot):
        return [pltpu.make_async_copy(hbm.at[expert], st.at[slot], sem.at[slot])
                for hbm, st in ((wg_hbm, wg_st), (wu_hbm, wu_st), (wd_hbm, wd_st))]

    @pl.when(used & (i == 0))
    def _():
        for c in weight_copies(te_ref[0], par_ref[0]):
            c.start()

    @pl.when(first)
    def _():
        slot = par_ref[i]
        for c in weight_copies(te_ref[i], slot):
            c.wait()

        @pl.when(hn_ref[i] == 1)
        def _():
            for c in weight_copies(ne_ref[i], 1 - slot):
                c.start()

        wgb_ref[...] = wg_st[slot].astype(BF16)
        wub_ref[...] = wu_st[slot].astype(BF16)
        wdb_ref[...] = wd_st[slot].astype(BF16)

    def expert_rows(rows):
        row = lax.broadcasted_iota(jnp.int32, (rows, 1), 0)
        x = jnp.where(row < tv_ref[i], _load_token_rows(x_ref, rows), 0.0).astype(BF16)
        g = jnp.dot(x, wgb_ref[...], preferred_element_type=F32) + bg_ref[0]
        u = jnp.dot(x, wub_ref[...], preferred_element_type=F32) + bu_ref[0]
        g = jnp.minimum(g, SWIGLU_LIMIT)
        u = jnp.clip(u, -SWIGLU_LIMIT, SWIGLU_LIMIT)
        h = g * (1.0 / (1.0 + jnp.exp(-SWIGLU_ALPHA * g))) * (u + 1.0)
        y = jnp.dot(h.astype(BF16), wdb_ref[...], preferred_element_type=F32) + bd_ref[0]
        for j in range(SUBLANES):
            o_ref[pl.ds(j, rows, stride=SUBLANES), :] = y[:, j * LANES:(j + 1) * LANES]
        if rows < tg:
            o_ref[rows * SUBLANES:, :] = jnp.zeros(((tg - rows) * SUBLANES, LANES), o_ref.dtype)

    half = tg // 2
    pl.when(used & (tv_ref[i] > half))(lambda: expert_rows(tg))
    pl.when(used & (tv_ref[i] <= half))(lambda: expert_rows(half))

    @pl.when(jnp.logical_not(used))
    def _():
        o_ref[...] = jnp.zeros(o_ref.shape, o_ref.dtype)


def _experts(xs, tile_expert, tile_valid, has_next, next_expert, tile_parity, n_used,
             wg, bg, wu, bu, wd, bd, tg):
    e, d, f = wg.shape
    n_tiles = xs.shape[0] // (tg * SUBLANES)
    xrow = lambda i, te, tv, hn, ne, par, nu: (jnp.minimum(i, nu[0] - 1), 0)
    bmap = lambda i, te, tv, hn, ne, par, nu: (te[i], 0, 0)
    hbm = pl.BlockSpec(memory_space=pl.ANY)
    grid_spec = pltpu.PrefetchScalarGridSpec(
        num_scalar_prefetch=6,
        grid=(n_tiles,),
        in_specs=[pl.BlockSpec((tg * SUBLANES, LANES), xrow),
                  hbm, pl.BlockSpec((1, 1, f), bmap),
                  hbm, pl.BlockSpec((1, 1, f), bmap),
                  hbm, pl.BlockSpec((1, 1, d), bmap)],
        out_specs=pl.BlockSpec((tg * SUBLANES, LANES), lambda i, te, tv, hn, ne, par, nu: (i, 0)),
        scratch_shapes=[pltpu.VMEM((2, d, f), F32), pltpu.VMEM((2, d, f), F32), pltpu.VMEM((2, f, d), F32),
                        pltpu.VMEM((d, f), BF16), pltpu.VMEM((d, f), BF16), pltpu.VMEM((f, d), BF16),
                        pltpu.SemaphoreType.DMA((2,))],
    )
    return pl.pallas_call(
        _experts_kernel,
        grid_spec=grid_spec,
        out_shape=jax.ShapeDtypeStruct(xs.shape, F32),
        compiler_params=_cparams("arbitrary"),
        name="experts",
    )(tile_expert, tile_valid, has_next, next_expert, tile_parity, n_used,
      xs, wg, bg.reshape(e, 1, f), wu, bu.reshape(e, 1, f), wd, bd.reshape(e, 1, d))


def _combine_kernel(cur_ref, nxt_ref, ys_hbm, x2t_ref, gate_ref, g_ref, b_ref, o_ref, buf0_ref, buf1_ref, sem,
                    *, alpha, n_steps):
    i = pl.program_id(0)
    tc = gate_ref.shape[0]
    n_copies = tc * TOP_K
    bufs = (buf0_ref, buf1_ref)

    def copy(slot_ref, b, t, k):
        return pltpu.make_async_copy(_token_tile(ys_hbm, slot_ref[k, t]),
                                     _token_tile(bufs[b].at[k], t), sem.at[b])

    @pl.when(i == 0)
    def _():
        def start(t, c):
            for k in range(TOP_K):
                copy(cur_ref, 0, t, k).start(priority=k % 2)
            return c
        lax.fori_loop(0, tc, start, 0, unroll=2)

    for b in range(2):
        @pl.when(i % 2 == b)
        def _(b=b):
            _wait_tiles(ys_hbm, bufs[b].at[0], sem.at[b], n_copies)
            for t in range(tc):
                for k in range(TOP_K):
                    copy(nxt_ref, 1 - b, t, k).start(priority=k % 2)
            gates = gate_ref[...]
            y = alpha * _load_token_rows(x2t_ref, tc)
            for k in range(TOP_K):
                y = y + gates[:, k:k + 1] * _load_token_rows(bufs[b].at[k], tc)
            o_ref[...] = _layer_norm(y, g_ref[...], b_ref[...])

            @pl.when(i == n_steps - 1)
            def _():
                _wait_tiles(ys_hbm, bufs[1 - b].at[0], sem.at[1 - b], n_copies)


def _combine(ys, slots, x2t, gates, g3, b3, tc, alpha):
    n = x2t.shape[0] // SUBLANES
    d = SUBLANES * LANES
    nt = n // tc
    const = lambda i: (0, 0)
    return pl.pallas_call(
        functools.partial(_combine_kernel, alpha=alpha, n_steps=nt),
        grid=(nt,),
        in_specs=[pl.BlockSpec((SUBLANES, tc), lambda i: (0, i), memory_space=pltpu.SMEM),
                  pl.BlockSpec((SUBLANES, tc), lambda i: (0, jnp.minimum(i + 1, nt - 1)),
                               memory_space=pltpu.SMEM),
                  pl.BlockSpec(memory_space=pl.ANY),
                  pl.BlockSpec((tc * SUBLANES, LANES), lambda i: (i, 0)),
                  pl.BlockSpec((tc, LANES), lambda i: (i, 0)),
                  pl.BlockSpec(g3.shape, const), pl.BlockSpec(b3.shape, const)],
        out_specs=pl.BlockSpec((tc, d), lambda i: (i, 0)),
        out_shape=jax.ShapeDtypeStruct((n, d), F32),
        scratch_shapes=[pltpu.VMEM((TOP_K, tc * SUBLANES, LANES), ys.dtype),
                        pltpu.VMEM((TOP_K, tc * SUBLANES, LANES), ys.dtype),
                        pltpu.SemaphoreType.DMA((2,))],
        compiler_params=_cparams("arbitrary"),
        name="combine",
    )(slots, slots, ys, x2t, gates, g3, b3)


def _rope_tables(seq, a_hd, b_hd):
    def cos_sin(pos, dim):
        inv = ROPE_THETA ** (-np.arange(0, dim, 2, dtype=np.float64) / dim)
        ang = pos.astype(np.float64)[:, None] * inv[None, :]
        return np.cos(ang).astype(np.float32), np.sin(ang).astype(np.float32)

    t = np.arange(seq, dtype=np.int32)
    cr, sr = cos_sin(t // GRID_W, a_hd // 2)
    cc, sc = cos_sin(t % GRID_W, a_hd // 2)
    cs, ss = cos_sin(t, b_hd)
    twice = lambda a: np.concatenate([a, a], axis=-1)
    cos_a = twice(np.concatenate([cr, cr, cc, cc], axis=-1))
    sin_a = twice(np.concatenate([-sr, sr, -sc, sc], axis=-1))
    cos_b = twice(np.concatenate([cs, cs], axis=-1))
    sin_b = twice(np.concatenate([-ss, ss], axis=-1))
    return tuple(jnp.asarray(a) for a in (cos_a, sin_a, cos_b, sin_b))


def _pick_tile(n, pref):
    t = min(pref, n)
    while n % t:
        t //= 2
    return t


def kernel(x, mem, w_in, a_q_norm, a_k_norm, b_lambda_q1, b_lambda_k1, b_lambda_q2, b_lambda_k2, b_subln, w_mix_out, ln1_g, ln1_b, w_mem_q, w_mem_kv, w_mem_out, ln2_g, ln2_b, w_router, b_router, w_e_gate, b_e_gate, w_e_up, b_e_up, w_e_down, b_e_down, ln3_g, ln3_b):
    batch, seq, d = x.shape
    depth = w_in.shape[0]
    n = batch * seq
    mem_tokens = mem.shape[1]
    n_experts = w_router.shape[-1]
    a_hd = a_q_norm.shape[-1]
    b_hd = b_lambda_q1.shape[-1]
    alpha = (2.0 * depth) ** 0.25

    a_q, a_kv = A_HEADS * a_hd, A_KV_HEADS * a_hd
    b_qk, b_v = B_HEADS * 2 * b_hd, B_HEADS * 2 * b_hd
    edges = [0, a_q, a_q + a_kv, a_q + 2 * a_kv, a_q + 2 * a_kv + b_qk,
             a_q + 2 * a_kv + 2 * b_qk, a_q + 2 * a_kv + 2 * b_qk + b_v]
    splits = tuple(zip(edges[:-1], edges[1:]))
    assert edges[-1] == w_in.shape[-1]

    tm = _pick_tile(seq, 1024)
    tq = _pick_tile(seq, 512)
    tg = 512
    tc = _pick_tile(n, 256)
    n_tiles = (n * TOP_K) // tg + n_experts
    cos_a, sin_a, cos_b, sin_b = _rope_tables(seq, a_hd, b_hd)
    seg = jnp.arange(LANES, dtype=jnp.int32) // a_hd
    bd = jnp.where(seg[:, None] == seg[None, :], 1.0 / a_hd, 0.0).astype(BF16)
    row2 = lambda v: v.reshape(1, -1)
    twice = lambda v: jnp.concatenate([v, v]).reshape(1, -1)

    h = x.reshape(n, d)
    mem2d = mem.reshape(batch * mem_tokens, d)
    for layer in range(depth):
        lambda_init = 0.8 - 0.6 * math.exp(-0.3 * layer)
        qa, ka, va, qb, kb, vb = _in_proj(h, w_in[layer].astype(BF16), splits, tm)
        oa = _attn_a(qa, ka, va, cos_a, sin_a, twice(a_q_norm[layer]), twice(a_k_norm[layer]), bd,
                     batch, seq, tq)
        ob = _attn_b(qb, kb, vb, cos_b, sin_b, row2(b_lambda_q1[layer]), row2(b_lambda_k1[layer]),
                     row2(b_lambda_q2[layer]), row2(b_lambda_k2[layer]), row2(b_subln[layer]),
                     batch, seq, tq, lambda_init)
        kvmem = _mem_kv(mem2d, w_mem_kv[layer].astype(BF16), _pick_tile(batch * mem_tokens, 512))
        x2t, code, gates, counts = _mix_xattn(
            oa, ob, h, w_mix_out[layer].astype(BF16), row2(ln1_g[layer]), row2(ln1_b[layer]),
            w_mem_q[layer].astype(BF16), kvmem, w_mem_out[layer].astype(BF16),
            row2(ln2_g[layer]), row2(ln2_b[layer]), w_router[layer].astype(BF16),
            row2(b_router[layer]), seq, mem_tokens, tm, alpha)
        slots, tile_expert, tile_valid, has_next, next_expert, tile_parity, n_used = _tile_map(
            counts.reshape(-1).astype(jnp.int32), code, tg, n_tiles)
        xs = _dispatch(x2t, slots, n_tiles * tg, _pick_tile(seq, 512))
        ys = _experts(xs, tile_expert, tile_valid, has_next, next_expert, tile_parity, n_used,
                      w_e_gate[layer], b_e_gate[layer], w_e_up[layer], b_e_up[layer],
                      w_e_down[layer], b_e_down[layer], tg)
        h = _combine(ys, slots, x2t, gates, row2(ln3_g[layer]), row2(ln3_b[layer]), tc, alpha)
    return h.reshape(batch, seq, d)
```

```python
import functools
import math

import jax
import jax.numpy as jnp
import numpy as np
from jax import lax
from jax.experimental import pallas as pl
from jax.experimental.pallas import tpu as pltpu

F32 = jnp.float32
BF16 = jnp.bfloat16

GRID_W = 64
ROPE_THETA = 10000.0
A_HEADS = 8
A_KV_HEADS = 2
A_GROUP = A_HEADS // A_KV_HEADS
B_HEADS = 4
MEM_HEADS = 4
TOP_K = 4
SWIGLU_LIMIT = 7.0
SWIGLU_ALPHA = 1.702
LN_EPS = 1e-5
RMS_EPS = 1e-6
LOG2E = 1.4426950408889634

VMEM_LIMIT_BYTES = 56 * 1024 * 1024
LANES = 128
SUBLANES = 8
CODE_EXPERT_SHIFT = 20
CODE_EXPERT_STRIDE = 1 << CODE_EXPERT_SHIFT
NT_DIMS = (((1,), (1,)), ((), ()))


def _cparams(*sem):
    return pltpu.CompilerParams(dimension_semantics=sem, vmem_limit_bytes=VMEM_LIMIT_BYTES)


def _layer_norm(x, g, b):
    mu = jnp.mean(x, axis=-1, keepdims=True)
    xc = x - mu
    var = jnp.mean(xc * xc, axis=-1, keepdims=True)
    return xc * lax.rsqrt(var + LN_EPS) * g + b


def _rms_norm(x, g):
    return x * lax.rsqrt(jnp.mean(x * x, axis=-1, keepdims=True) + RMS_EPS) * g


def _rope(x, cos, sin_signed, half):
    w = x.shape[-1]
    lane = lax.broadcasted_iota(jnp.int32, (1, w), 1)
    first = (lane & (2 * half - 1)) < half
    partner = jnp.where(first, pltpu.roll(x, w - half, 1), pltpu.roll(x, half, 1))
    return x * cos + partner * sin_signed


def _segment_mean(x, bd):
    hi = x.astype(BF16)
    lo = (x - hi.astype(F32)).astype(BF16)
    return (jnp.dot(hi, bd, preferred_element_type=F32) + jnp.dot(lo, bd, preferred_element_type=F32))


def _swap_lane_halves(x):
    return pltpu.roll(x, x.shape[-1] // 2, 1)


def _in_proj_kernel(x_ref, w_ref, *out_refs, splits):
    xb = x_ref[...].astype(BF16)
    for o_ref, (c0, c1) in zip(out_refs, splits):
        o_ref[...] = jnp.dot(xb, w_ref[:, c0:c1], preferred_element_type=F32).astype(o_ref.dtype)


def _in_proj(x2d, w_bf, splits, tm):
    n, d = x2d.shape
    widths = [c1 - c0 for c0, c1 in splits]
    return pl.pallas_call(
        functools.partial(_in_proj_kernel, splits=splits),
        grid=(n // tm,),
        in_specs=[pl.BlockSpec((tm, d), lambda i: (i, 0)),
                  pl.BlockSpec(w_bf.shape, lambda i: (0, 0))],
        out_specs=[pl.BlockSpec((tm, w), lambda i: (i, 0)) for w in widths],
        out_shape=[jax.ShapeDtypeStruct((n, w), BF16) for w in widths],
        compiler_params=_cparams("parallel"),
        name="in_proj",
    )(x2d, w_bf)


def _attn_a_kernel(q_ref, k_ref, v_ref, cq_ref, sq_ref, ck_ref, sk_ref, gq_ref, gk_ref, bd_ref,
                   o_ref, kn_ref, va_ref, *, hd):
    low = lax.broadcasted_iota(jnp.int32, (1, 2 * hd), 1) < hd

    @pl.when(pl.program_id(1) == 0)
    def _():
        k = k_ref[...].astype(F32)
        k = k * lax.rsqrt(_segment_mean(k * k, bd_ref[...]) + RMS_EPS) * gk_ref[...]
        k = _rope(k, ck_ref[...], sk_ref[...], hd // 4)
        k_sw = _swap_lane_halves(k)
        v = v_ref[...].astype(F32)
        kn_ref[0] = jnp.where(low, k, k_sw).astype(BF16)
        kn_ref[1] = jnp.where(low, k_sw, k).astype(BF16)
        va_ref[0] = jnp.where(low, v, 1.0).astype(BF16)
        va_ref[1] = jnp.where(low, _swap_lane_halves(v), 1.0).astype(BF16)

    scale = hd ** -0.5 * LOG2E

    n_pairs = A_HEADS // 2
    qs = []
    for pair in range(n_pairs):
        q = q_ref[:, pair * 2 * hd:(pair + 1) * 2 * hd].astype(F32)
        q = q * lax.rsqrt(_segment_mean(q * q, bd_ref[...]) + RMS_EPS) * gq_ref[...]
        q = _rope(q, cq_ref[...], sq_ref[...], hd // 4) * scale
        qs.append([jnp.where(keep, q, 0.0).astype(BF16) for keep in (low, jnp.logical_not(low))])

    def scores(pair):
        kv = (2 * pair) // A_GROUP
        return [lax.dot_general(qh, kn_ref[kv], NT_DIMS, preferred_element_type=F32) for qh in qs[pair]]

    outs = []
    s_next = scores(0)
    for pair in range(n_pairs):
        s_cur = s_next
        if pair + 1 < n_pairs:
            s_next = scores(pair + 1)
        kv = (2 * pair) // A_GROUP
        o_pair = []
        for s in s_cur:
            e = jnp.exp2(s - jnp.max(s, axis=-1, keepdims=True)).astype(BF16)
            oa = jnp.dot(e, va_ref[kv], preferred_element_type=F32)
            l = jnp.where(low, _swap_lane_halves(oa), oa)
            o_pair.append(oa * (1.0 / l))
        outs.append(jnp.where(low, o_pair[0], _swap_lane_halves(o_pair[1])))
    o_ref[...] = jnp.concatenate(outs, axis=-1).astype(o_ref.dtype)


def _attn_a(qa, ka, va, cos, sin, gq, gk, bd, batch, seq, tq):
    n, dq = qa.shape
    hd = dq // A_HEADS
    assert 2 * hd == LANES and ka.shape[1] == LANES and va.shape[1] == LANES
    nq = seq // tq
    row = lambda b, i: (b * nq + i, 0)
    per_b = lambda b, i: (b, 0)
    const = lambda b, i: (0, 0)
    return pl.pallas_call(
        functools.partial(_attn_a_kernel, hd=hd),
        grid=(batch, nq),
        in_specs=[pl.BlockSpec((tq, dq), row),
                  pl.BlockSpec((seq, LANES), per_b),
                  pl.BlockSpec((seq, LANES), per_b),
                  pl.BlockSpec((tq, LANES), lambda b, i: (i, 0)),
                  pl.BlockSpec((tq, LANES), lambda b, i: (i, 0)),
                  pl.BlockSpec((seq, LANES), const),
                  pl.BlockSpec((seq, LANES), const),
                  pl.BlockSpec((1, LANES), const),
                  pl.BlockSpec((1, LANES), const),
                  pl.BlockSpec((LANES, LANES), const)],
        out_specs=pl.BlockSpec((tq, dq), row),
        out_shape=jax.ShapeDtypeStruct((n, dq), BF16),
        scratch_shapes=[pltpu.VMEM((A_KV_HEADS, seq, LANES), BF16),
                        pltpu.VMEM((A_KV_HEADS, seq, LANES), BF16)],
        compiler_params=_cparams("arbitrary", "arbitrary"),
        name="attn_a",
    )(qa, ka, va, cos, sin, cos, sin, gq, gk, bd)


def _attn_b_kernel(q_ref, k_ref, v_ref, cq_ref, sq_ref, ck_ref, sk_ref,
                   lq1_ref, lk1_ref, lq2_ref, lk2_ref, g_ref, o_ref, kr_ref, va_ref, *, hd, lambda_init):
    low = lax.broadcasted_iota(jnp.int32, (1, 2 * hd), 1) < hd
    vd = 2 * hd

    @pl.when(pl.program_id(1) == 0)
    def _():
        ones = jnp.ones((v_ref.shape[0], vd), BF16)
        for h in range(B_HEADS):
            k = _rope(k_ref[:, h * 2 * hd:(h + 1) * 2 * hd].astype(F32), ck_ref[...], sk_ref[...], hd // 2)
            k_sw = _swap_lane_halves(k)
            kr_ref[2 * h] = jnp.where(low, k, k_sw).astype(BF16)
            kr_ref[2 * h + 1] = jnp.where(low, k_sw, k).astype(BF16)
            va_ref[h] = jnp.concatenate([v_ref[:, h * vd:(h + 1) * vd], ones], axis=-1)

    lam = (jnp.exp(jnp.sum(lq1_ref[...] * lk1_ref[...], axis=-1, keepdims=True))
           - jnp.exp(jnp.sum(lq2_ref[...] * lk2_ref[...], axis=-1, keepdims=True))
           + lambda_init)
    scale = hd ** -0.5 * LOG2E
    qs = []
    for h in range(B_HEADS):
        q = q_ref[:, h * 2 * hd:(h + 1) * 2 * hd].astype(F32)
        q = _rope(q, cq_ref[...], sq_ref[...], hd // 2) * scale
        qs.append([jnp.where(keep, q, 0.0).astype(BF16) for keep in (low, jnp.logical_not(low))])

    def scores(h):
        return [lax.dot_general(qm, kr_ref[2 * h + c], NT_DIMS, preferred_element_type=F32)
                for c, qm in enumerate(qs[h])]

    outs = []
    s_next = scores(0)
    for h in range(B_HEADS):
        s_cur = s_next
        if h + 1 < B_HEADS:
            s_next = scores(h + 1)
        oa = [jnp.dot(jnp.exp2(s - jnp.max(s, axis=-1, keepdims=True)).astype(BF16), va_ref[h],
                      preferred_element_type=F32) for s in s_cur]
        o = oa[0][:, :vd] * (1.0 / oa[0][:, vd:]) - oa[1][:, :vd] * (lam / oa[1][:, vd:])
        outs.append(_rms_norm(o, g_ref[...]) * (1.0 - lambda_init))
    o_ref[...] = jnp.concatenate(outs, axis=-1).astype(o_ref.dtype)


def _attn_b(qb, kb, vb, cos, sin, lq1, lk1, lq2, lk2, subln, batch, seq, tq, lambda_init):
    n, dq = qb.shape
    hd = dq // (2 * B_HEADS)
    assert 2 * hd == LANES and vb.shape[1] == B_HEADS * LANES
    nq = seq // tq
    row = lambda b, i: (b * nq + i, 0)
    per_b = lambda b, i: (b, 0)
    const = lambda b, i: (0, 0)
    vec = pl.BlockSpec((1, hd), const)
    return pl.pallas_call(
        functools.partial(_attn_b_kernel, hd=hd, lambda_init=lambda_init),
        grid=(batch, nq),
        in_specs=[pl.BlockSpec((tq, dq), row),
                  pl.BlockSpec((seq, dq), per_b),
                  pl.BlockSpec((seq, vb.shape[1]), per_b),
                  pl.BlockSpec((tq, LANES), lambda b, i: (i, 0)),
                  pl.BlockSpec((tq, LANES), lambda b, i: (i, 0)),
                  pl.BlockSpec((seq, LANES), const),
                  pl.BlockSpec((seq, LANES), const),
                  vec, vec, vec, vec,
                  pl.BlockSpec((1, LANES), const)],
        out_specs=pl.BlockSpec((tq, vb.shape[1]), row),
        out_shape=jax.ShapeDtypeStruct((n, vb.shape[1]), BF16),
        scratch_shapes=[pltpu.VMEM((2 * B_HEADS, seq, LANES), BF16),
                        pltpu.VMEM((B_HEADS, seq, 2 * LANES), BF16)],
        compiler_params=_cparams("arbitrary", "arbitrary"),
        name="attn_b",
    )(qb, kb, vb, cos, sin, cos, sin, lq1, lk1, lq2, lk2, subln)


def _matmul_kernel(x_ref, w_ref, o_ref):
    o_ref[...] = jnp.dot(x_ref[...].astype(BF16), w_ref[...],
                         preferred_element_type=F32).astype(o_ref.dtype)


def _mem_kv(mem2d, w_bf, tm):
    n, d = mem2d.shape
    return pl.pallas_call(
        _matmul_kernel,
        grid=(n // tm,),
        in_specs=[pl.BlockSpec((tm, d), lambda i: (i, 0)),
                  pl.BlockSpec(w_bf.shape, lambda i: (0, 0))],
        out_specs=pl.BlockSpec((tm, w_bf.shape[1]), lambda i: (i, 0)),
        out_shape=jax.ShapeDtypeStruct((n, w_bf.shape[1]), BF16),
        compiler_params=_cparams("parallel"),
        name="mem_kv",
    )(mem2d, w_bf)


def _mix_xattn_kernel(oa_ref, ob_ref, x_ref, wmix_ref, g1_ref, b1_ref, wq_ref, kv_ref, wo_ref,
                      g2_ref, b2_ref, wr_ref, br_ref,
                      x2t_ref, code_ref, gate_ref, cnt_ref, carry_ref, *, alpha, n_experts, sub):
    @pl.when(pl.program_id(0) == 0)
    def _():
        carry_ref[...] = jnp.zeros(carry_ref.shape, carry_ref.dtype)

    da = oa_ref.shape[1]
    d = x_ref.shape[1]
    hd = d // MEM_HEADS
    scale = hd ** -0.5
    starts = list(range(0, x_ref.shape[0], sub))
    subs = [pl.ds(r0, sub) for r0 in starts]

    mix = [jnp.dot(oa_ref[rows, :], wmix_ref[:da, :], preferred_element_type=F32)
           + jnp.dot(ob_ref[rows, :], wmix_ref[da:, :], preferred_element_type=F32) for rows in subs]
    x1 = [_layer_norm(alpha * x_ref[rows, :] + m, g1_ref[...], b1_ref[...]) for rows, m in zip(subs, mix)]
    q = [jnp.dot(v.astype(BF16), wq_ref[...], preferred_element_type=F32).astype(BF16) for v in x1]

    def mem_attention(qv):
        outs = []
        for h in range(MEM_HEADS):
            s = lax.dot_general(qv[:, h * hd:(h + 1) * hd], kv_ref[:, h * hd:(h + 1) * hd],
                                NT_DIMS, preferred_element_type=F32) * scale
            e = jnp.exp(s - jnp.max(s, axis=-1, keepdims=True))
            l = jnp.sum(e, axis=-1, keepdims=True)
            o = jnp.dot(e.astype(BF16), kv_ref[:, d + h * hd:d + (h + 1) * hd],
                        preferred_element_type=F32)
            outs.append((o * (1.0 / l)).astype(BF16))
        return jnp.concatenate(outs, axis=-1)

    att = [mem_attention(qv) for qv in q]
    xa = [jnp.dot(a, wo_ref[...], preferred_element_type=F32) for a in att]
    x2 = [_layer_norm(alpha * v + a, g2_ref[...], b2_ref[...]) for v, a in zip(x1, xa)]
    for r0, v in zip(starts, x2):
        for j in range(SUBLANES):
            x2t_ref[pl.ds(r0 * SUBLANES + j, sub, stride=SUBLANES), :] = v[:, j * LANES:(j + 1) * LANES]
    all_logits = [jnp.dot(v.astype(BF16), wr_ref[...], preferred_element_type=F32) + br_ref[...] for v in x2]

    tm = x_ref.shape[0]
    lt = jnp.concatenate(all_logits, axis=0).T[:n_experts, :]
    row = lax.broadcasted_iota(jnp.int32, (n_experts, tm), 0).astype(F32)
    vals, idxs = [], []
    for _ in range(TOP_K):
        m = jnp.max(lt, axis=0, keepdims=True)
        idx = jnp.min(jnp.where(lt == m, row, float(n_experts)), axis=0, keepdims=True)
        vals.append(m)
        idxs.append(idx)
        lt = jnp.where(row == idx, -jnp.inf, lt)
    es = [jnp.exp(v - vals[0]) for v in vals]
    den = es[0] + es[1] + es[2] + es[3]

    chosen = [row == idx for idx in idxs]
    cnt = sum(c.astype(F32) for c in chosen)
    r_id = lax.broadcasted_iota(jnp.int32, (sub, sub), 0)
    c_id = lax.broadcasted_iota(jnp.int32, (sub, sub), 1)
    upper = jnp.where(r_id < c_id, 1.0, 0.0).astype(BF16)
    carry = carry_ref[...]
    before = []
    for r0 in starts:
        cnt_s = cnt[:, r0:r0 + sub]
        before.append(carry + jnp.dot(cnt_s.astype(BF16), upper, preferred_element_type=F32))
        carry = carry + jnp.sum(cnt_s, axis=1, keepdims=True)
    carry_ref[...] = carry
    cnt_ref[...] = carry
    before = jnp.concatenate(before, axis=1)

    srow = lax.broadcasted_iota(jnp.int32, (SUBLANES, tm), 0)
    grow = lax.broadcasted_iota(jnp.int32, (LANES, tm), 0)
    code_out = jnp.zeros((SUBLANES, tm), jnp.int32)
    gate_out = jnp.zeros((LANES, tm), F32)
    for k in range(TOP_K):
        rank = jnp.sum(jnp.where(chosen[k], before, 0.0), axis=0, keepdims=True)
        code = idxs[k].astype(jnp.int32) * CODE_EXPERT_STRIDE + rank.astype(jnp.int32)
        code_out = jnp.where(srow == k, code, code_out)
        gate_out = jnp.where(grow == k, es[k] / den, gate_out)
    code_ref[...] = code_out
    gate_ref[...] = gate_out.T


def _mix_xattn(oa, ob, x2d, wmix, g1, b1, wq, kvmem, wo, g2, b2, wr, br, seq, mem_tokens, tm, alpha):
    n, d = x2d.shape
    tiles_per_b = seq // tm
    row = lambda i: (i, 0)
    const = lambda i: (0, 0)
    full = lambda a: pl.BlockSpec(a.shape, const)
    n_experts = wr.shape[1]
    assert d == SUBLANES * LANES and n < CODE_EXPERT_STRIDE and n_experts <= LANES
    wr = jnp.pad(wr, ((0, 0), (0, LANES - n_experts)))
    br = jnp.pad(br, ((0, 0), (0, LANES - n_experts)), constant_values=-jnp.inf)
    return pl.pallas_call(
        functools.partial(_mix_xattn_kernel, alpha=alpha, n_experts=n_experts, sub=_pick_tile(tm, 256)),
        grid=(n // tm,),
        in_specs=[pl.BlockSpec((tm, oa.shape[1]), row),
                  pl.BlockSpec((tm, ob.shape[1]), row),
                  pl.BlockSpec((tm, d), row),
                  full(wmix), full(g1), full(b1), full(wq),
                  pl.BlockSpec((mem_tokens, kvmem.shape[1]), lambda i: (i // tiles_per_b, 0)),
                  full(wo), full(g2), full(b2), full(wr), full(br)],
        out_specs=[pl.BlockSpec((tm * SUBLANES, LANES), row),
                   pl.BlockSpec((SUBLANES, tm), lambda i: (0, i)), pl.BlockSpec((tm, LANES), row),
                   pl.BlockSpec((n_experts, 1), const)],
        out_shape=[jax.ShapeDtypeStruct((n * SUBLANES, LANES), F32),
                   jax.ShapeDtypeStruct((SUBLANES, n), jnp.int32),
                   jax.ShapeDtypeStruct((n, LANES), F32),
                   jax.ShapeDtypeStruct((n_experts, 1), F32)],
        scratch_shapes=[pltpu.VMEM((n_experts, 1), F32)],
        compiler_params=_cparams("arbitrary"),
        name="mix_xattn",
    )(oa, ob, x2d, wmix, g1, b1, wq, kvmem, wo, g2, b2, wr, br)


def _tile_map(counts, code4, tg, n_tiles):
    n_experts = counts.shape[0]
    experts = jnp.arange(n_experts, dtype=jnp.int32)
    padded = ((counts + tg - 1) // tg) * tg
    pad_end = jnp.cumsum(padded)
    pad_start = pad_end - padded
    n_used = pad_end[-1] // tg

    pair_expert = lax.shift_right_logical(code4, CODE_EXPERT_SHIFT)
    pair_start = jnp.sum(jnp.where(pair_expert[..., None] == experts, pad_start, 0), axis=-1)
    slots = pair_start + (code4 & (CODE_EXPERT_STRIDE - 1))

    tile_id = jnp.arange(n_tiles, dtype=jnp.int32)
    tile_start = tile_id * tg
    tile_expert = jnp.minimum(
        jnp.sum((tile_start[:, None] >= pad_end[None, :]).astype(jnp.int32), axis=1), n_experts - 1)
    onehot = (tile_expert[:, None] == experts[None, :]).astype(jnp.int32)
    pick = lambda per_expert: jnp.sum(onehot * per_expert[None, :], axis=1)
    tile_valid = jnp.clip(pick(pad_start + counts) - tile_start, 0, tg)
    next_tile = pick(pad_end) // tg
    has_next = (next_tile < n_used).astype(jnp.int32)
    next_expert = jnp.sum((tile_id[:, None] == jnp.minimum(next_tile, n_tiles - 1)[None, :]).astype(jnp.int32)
                          * tile_expert[:, None], axis=0)
    nonempty_before = jnp.cumsum((counts > 0).astype(jnp.int32)) - (counts > 0).astype(jnp.int32)
    tile_parity = pick(nonempty_before) & 1
    i32 = lambda a: a.astype(jnp.int32)
    return (i32(slots), i32(tile_expert), i32(tile_valid), i32(has_next), i32(next_expert),
            i32(tile_parity), i32(n_used).reshape(1))


def _token_tile(ref, index):
    if isinstance(index, int):
        return ref.at[pl.ds(index * SUBLANES, SUBLANES)]
    return ref.at[pl.ds(pl.multiple_of(index * SUBLANES, SUBLANES), SUBLANES)]


def _wait_tiles(src_ref, dst_ref, sem, count):
    def wait(_, c):
        pltpu.make_async_copy(_token_tile(src_ref, 0), _token_tile(dst_ref, 0), sem).wait()
        return c
    lax.fori_loop(0, count, wait, 0, unroll=8)


def _load_token_rows(ref, rows):
    return jnp.concatenate([ref[pl.ds(j, rows, stride=SUBLANES), :] for j in range(SUBLANES)], axis=-1)


def _dispatch_kernel(slot_ref, x_ref, o_hbm, sem, *, tokens):
    def start(t, c):
        for k in range(TOP_K):
            pltpu.make_async_copy(_token_tile(x_ref, t), _token_tile(o_hbm, slot_ref[k, t]),
                                  sem).start(priority=k % 2)
        return c
    lax.fori_loop(0, tokens, start, 0, unroll=2)
    _wait_tiles(x_ref, o_hbm, sem, tokens * TOP_K)


def _dispatch(x2t, slots, n_slots, tm):
    n = x2t.shape[0] // SUBLANES
    nt = n // tm
    return pl.pallas_call(
        functools.partial(_dispatch_kernel, tokens=tm),
        grid=(nt,),
        in_specs=[pl.BlockSpec((SUBLANES, tm), lambda i: (0, i), memory_space=pltpu.SMEM),
                  pl.BlockSpec((tm * SUBLANES, LANES), lambda i: (i, 0))],
        out_specs=pl.BlockSpec(memory_space=pl.ANY),
        out_shape=jax.ShapeDtypeStruct((n_slots * SUBLANES, LANES), x2t.dtype),
        scratch_shapes=[pltpu.SemaphoreType.DMA(())],
        compiler_params=_cparams("arbitrary"),
        name="dispatch",
    )(slots, x2t)


def _experts_kernel(te_ref, tv_ref, hn_ref, ne_ref, par_ref, nused_ref,
                    x_ref, wg_hbm, bg_ref, wu_hbm, bu_ref, wd_hbm, bd_ref,
                    o_ref, wg_st, wu_st, wd_st, wgb_ref, wub_ref, wdb_ref, sem):
    i = pl.program_id(0)
    used = i < nused_ref[0]
    first = used & ((i == 0) | (te_ref[i] != te_ref[jnp.maximum(i - 1, 0)]))
    tg = x_ref.shape[0] // SUBLANES

    def weight_copies(expert, slot):
        return [pltpu.make_async_copy(hbm.at[expert], st.at[slot], sem.at[slot])
                for hbm, st in ((wg_hbm, wg_st), (wu_hbm, wu_st), (wd_hbm, wd_st))]

    @pl.when(used & (i == 0))
    def _():
        for c in weight_copies(te_ref[0], par_ref[0]):
            c.start()

    @pl.when(first)
    def _():
        slot = par_ref[i]
        for c in weight_copies(te_ref[i], slot):
            c.wait()

        @pl.when(hn_ref[i] == 1)
        def _():
            for c in weight_copies(ne_ref[i], 1 - slot):
                c.start()

        wgb_ref[...] = wg_st[slot].astype(BF16)
        wub_ref[...] = wu_st[slot].astype(BF16)
        wdb_ref[...] = wd_st[slot].astype(BF16)

    def expert_rows(rows):
        row = lax.broadcasted_iota(jnp.int32, (rows, 1), 0)
        x = jnp.where(row < tv_ref[i], _load_token_rows(x_ref, rows), 0.0).astype(BF16)
        g = jnp.dot(x, wgb_ref[...], preferred_element_type=F32) + bg_ref[0]
        u = jnp.dot(x, wub_ref[...], preferred_element_type=F32) + bu_ref[0]
        g = jnp.minimum(g, SWIGLU_LIMIT)
        u = jnp.clip(u, -SWIGLU_LIMIT, SWIGLU_LIMIT)
        h = g * (1.0 / (1.0 + jnp.exp(-SWIGLU_ALPHA * g))) * (u + 1.0)
        y = jnp.dot(h.astype(BF16), wdb_ref[...], preferred_element_type=F32) + bd_ref[0]
        for j in range(SUBLANES):
            o_ref[pl.ds(j, rows, stride=SUBLANES), :] = y[:, j * LANES:(j + 1) * LANES]
        if rows < tg:
            o_ref[rows * SUBLANES:, :] = jnp.zeros(((tg - rows) * SUBLANES, LANES), o_ref.dtype)

    half = tg // 2
    pl.when(used & (tv_ref[i] > half))(lambda: expert_rows(tg))
    pl.when(used & (tv_ref[i] <= half))(lambda: expert_rows(half))

    @pl.when(jnp.logical_not(used))
    def _():
        o_ref[...] = jnp.zeros(o_ref.shape, o_ref.dtype)


def _experts(xs, tile_expert, tile_valid, has_next, next_expert, tile_parity, n_used,
             wg, bg, wu, bu, wd, bd, tg):
    e, d, f = wg.shape
    n_tiles = xs.shape[0] // (tg * SUBLANES)
    xrow = lambda i, te, tv, hn, ne, par, nu: (jnp.minimum(i, nu[0] - 1), 0)
    bmap = lambda i, te, tv, hn, ne, par, nu: (te[i], 0, 0)
    hbm = pl.BlockSpec(memory_space=pl.ANY)
    grid_spec = pltpu.PrefetchScalarGridSpec(
        num_scalar_prefetch=6,
        grid=(n_tiles,),
        in_specs=[pl.BlockSpec((tg * SUBLANES, LANES), xrow),
                  hbm, pl.BlockSpec((1, 1, f), bmap),
                  hbm, pl.BlockSpec((1, 1, f), bmap),
                  hbm, pl.BlockSpec((1, 1, d), bmap)],
        out_specs=pl.BlockSpec((tg * SUBLANES, LANES), lambda i, te, tv, hn, ne, par, nu: (i, 0)),
        scratch_shapes=[pltpu.VMEM((2, d, f), F32), pltpu.VMEM((2, d, f), F32), pltpu.VMEM((2, f, d), F32),
                        pltpu.VMEM((d, f), BF16), pltpu.VMEM((d, f), BF16), pltpu.VMEM((f, d), BF16),
                        pltpu.SemaphoreType.DMA((2,))],
    )
    return pl.pallas_call(
        _experts_kernel,
        grid_spec=grid_spec,
        out_shape=jax.ShapeDtypeStruct(xs.shape, F32),
        compiler_params=_cparams("arbitrary"),
        name="experts",
    )(tile_expert, tile_valid, has_next, next_expert, tile_parity, n_used,
      xs, wg, bg.reshape(e, 1, f), wu, bu.reshape(e, 1, f), wd, bd.reshape(e, 1, d))


def _combine_kernel(cur_ref, nxt_ref, ys_hbm, x2t_ref, gate_ref, g_ref, b_ref, o_ref, buf0_ref, buf1_ref, sem,
                    *, alpha, n_steps):
    i = pl.program_id(0)
    tc = gate_ref.shape[0]
    n_copies = tc * TOP_K
    bufs = (buf0_ref, buf1_ref)

    def copy(slot_ref, b, t, k):
        return pltpu.make_async_copy(_token_tile(ys_hbm, slot_ref[k, t]),
                                     _token_tile(bufs[b].at[k], t), sem.at[b])

    @pl.when(i == 0)
    def _():
        def start(t, c):
            for k in range(TOP_K):
                copy(cur_ref, 0, t, k).start(priority=k % 2)
            return c
        lax.fori_loop(0, tc, start, 0, unroll=2)

    for b in range(2):
        @pl.when(i % 2 == b)
        def _(b=b):
            _wait_tiles(ys_hbm, bufs[b].at[0], sem.at[b], n_copies)
            for t in range(tc):
                for k in range(TOP_K):
                    copy(nxt_ref, 1 - b, t, k).start(priority=k % 2)
            gates = gate_ref[...]
            y = alpha * _load_token_rows(x2t_ref, tc)
            for k in range(TOP_K):
                y = y + gates[:, k:k + 1] * _load_token_rows(bufs[b].at[k], tc)
            o_ref[...] = _layer_norm(y, g_ref[...], b_ref[...])

            @pl.when(i == n_steps - 1)
            def _():
                _wait_tiles(ys_hbm, bufs[1 - b].at[0], sem.at[1 - b], n_copies)


def _combine(ys, slots, x2t, gates, g3, b3, tc, alpha):
    n = x2t.shape[0] // SUBLANES
    d = SUBLANES * LANES
    nt = n // tc
    const = lambda i: (0, 0)
    return pl.pallas_call(
        functools.partial(_combine_kernel, alpha=alpha, n_steps=nt),
        grid=(nt,),
        in_specs=[pl.BlockSpec((SUBLANES, tc), lambda i: (0, i), memory_space=pltpu.SMEM),
                  pl.BlockSpec((SUBLANES, tc), lambda i: (0, jnp.minimum(i + 1, nt - 1)),
                               memory_space=pltpu.SMEM),
                  pl.BlockSpec(memory_space=pl.ANY),
                  pl.BlockSpec((tc * SUBLANES, LANES), lambda i: (i, 0)),
                  pl.BlockSpec((tc, LANES), lambda i: (i, 0)),
                  pl.BlockSpec(g3.shape, const), pl.BlockSpec(b3.shape, const)],
        out_specs=pl.BlockSpec((tc, d), lambda i: (i, 0)),
        out_shape=jax.ShapeDtypeStruct((n, d), F32),
        scratch_shapes=[pltpu.VMEM((TOP_K, tc * SUBLANES, LANES), ys.dtype),
                        pltpu.VMEM((TOP_K, tc * SUBLANES, LANES), ys.dtype),
                        pltpu.SemaphoreType.DMA((2,))],
        compiler_params=_cparams("arbitrary"),
        name="combine",
    )(slots, slots, ys, x2t, gates, g3, b3)


def _rope_tables(seq, a_hd, b_hd):
    def cos_sin(pos, dim):
        inv = ROPE_THETA ** (-np.arange(0, dim, 2, dtype=np.float64) / dim)
        ang = pos.astype(np.float64)[:, None] * inv[None, :]
        return np.cos(ang).astype(np.float32), np.sin(ang).astype(np.float32)

    t = np.arange(seq, dtype=np.int32)
    cr, sr = cos_sin(t // GRID_W, a_hd // 2)
    cc, sc = cos_sin(t % GRID_W, a_hd // 2)
    cs, ss = cos_sin(t, b_hd)
    twice = lambda a: np.concatenate([a, a], axis=-1)
    cos_a = twice(np.concatenate([cr, cr, cc, cc], axis=-1))
    sin_a = twice(np.concatenate([-sr, sr, -sc, sc], axis=-1))
    cos_b = twice(np.concatenate([cs, cs], axis=-1))
    sin_b = twice(np.concatenate([-ss, ss], axis=-1))
    return tuple(jnp.asarray(a) for a in (cos_a, sin_a, cos_b, sin_b))


def _pick_tile(n, pref):
    t = min(pref, n)
    while n % t:
        t //= 2
    return t


def kernel(x, mem, w_in, a_q_norm, a_k_norm, b_lambda_q1, b_lambda_k1, b_lambda_q2, b_lambda_k2, b_subln, w_mix_out, ln1_g, ln1_b, w_mem_q, w_mem_kv, w_mem_out, ln2_g, ln2_b, w_router, b_router, w_e_gate, b_e_gate, w_e_up, b_e_up, w_e_down, b_e_down, ln3_g, ln3_b):
    batch, seq, d = x.shape
    depth = w_in.shape[0]
    n = batch * seq
    mem_tokens = mem.shape[1]
    n_experts = w_router.shape[-1]
    a_hd = a_q_norm.shape[-1]
    b_hd = b_lambda_q1.shape[-1]
    alpha = (2.0 * depth) ** 0.25

    a_q, a_kv = A_HEADS * a_hd, A_KV_HEADS * a_hd
    b_qk, b_v = B_HEADS * 2 * b_hd, B_HEADS * 2 * b_hd
    edges = [0, a_q, a_q + a_kv, a_q + 2 * a_kv, a_q + 2 * a_kv + b_qk,
             a_q + 2 * a_kv + 2 * b_qk, a_q + 2 * a_kv + 2 * b_qk + b_v]
    splits = tuple(zip(edges[:-1], edges[1:]))
    assert edges[-1] == w_in.shape[-1]

    tm = _pick_tile(seq, 1024)
    tq = _pick_tile(seq, 512)
    tg = 512
    tc = _pick_tile(n, 256)
    n_tiles = (n * TOP_K) // tg + n_experts
    cos_a, sin_a, cos_b, sin_b = _rope_tables(seq, a_hd, b_hd)
    seg = jnp.arange(LANES, dtype=jnp.int32) // a_hd
    bd = jnp.where(seg[:, None] == seg[None, :], 1.0 / a_hd, 0.0).astype(BF16)
    row2 = lambda v: v.reshape(1, -1)
    twice = lambda v: jnp.concatenate([v, v]).reshape(1, -1)

    h = x.reshape(n, d)
    mem2d = mem.reshape(batch * mem_tokens, d)
    for layer in range(depth):
        lambda_init = 0.8 - 0.6 * math.exp(-0.3 * layer)
        qa, ka, va, qb, kb, vb = _in_proj(h, w_in[layer].astype(BF16), splits, tm)
        oa = _attn_a(qa, ka, va, cos_a, sin_a, twice(a_q_norm[layer]), twice(a_k_norm[layer]), bd,
                     batch, seq, tq)
        ob = _attn_b(qb, kb, vb, cos_b, sin_b, row2(b_lambda_q1[layer]), row2(b_lambda_k1[layer]),
                     row2(b_lambda_q2[layer]), row2(b_lambda_k2[layer]), row2(b_subln[layer]),
                     batch, seq, tq, lambda_init)
        kvmem = _mem_kv(mem2d, w_mem_kv[layer].astype(BF16), _pick_tile(batch * mem_tokens, 512))
        x2t, code, gates, counts = _mix_xattn(
            oa, ob, h, w_mix_out[layer].astype(BF16), row2(ln1_g[layer]), row2(ln1_b[layer]),
            w_mem_q[layer].astype(BF16), kvmem, w_mem_out[layer].astype(BF16),
            row2(ln2_g[layer]), row2(ln2_b[layer]), w_router[layer].astype(BF16),
            row2(b_router[layer]), seq, mem_tokens, tm, alpha)
        slots, tile_expert, tile_valid, has_next, next_expert, tile_parity, n_used = _tile_map(
            counts.reshape(-1).astype(jnp.int32), code, tg, n_tiles)
        xs = _dispatch(x2t, slots, n_tiles * tg, _pick_tile(seq, 512))
        ys = _experts(xs, tile_expert, tile_valid, has_next, next_expert, tile_parity, n_used,
                      w_e_gate[layer], b_e_gate[layer], w_e_up[layer], b_e_up[layer],
                      w_e_down[layer], b_e_down[layer], tg)
        h = _combine(ys, slots, x2t, gates, row2(ln3_g[layer]), row2(ln3_b[layer]), tc, alpha)
    return h.reshape(batch, seq, d)
```

```python
import functools
import math

import jax
import jax.numpy as jnp
import numpy as np
from jax import lax
from jax.experimental import pallas as pl
from jax.experimental.pallas import tpu as pltpu

F32 = jnp.float32
BF16 = jnp.bfloat16

GRID_W = 64
ROPE_THETA = 10000.0
A_HEADS = 8
A_KV_HEADS = 2
A_GROUP = A_HEADS // A_KV_HEADS
B_HEADS = 4
MEM_HEADS = 4
TOP_K = 4
SWIGLU_LIMIT = 7.0
SWIGLU_ALPHA = 1.702
LN_EPS = 1e-5
RMS_EPS = 1e-6
LOG2E = 1.4426950408889634

VMEM_LIMIT_BYTES = 56 * 1024 * 1024
LANES = 128
SUBLANES = 8
CODE_EXPERT_SHIFT = 20
CODE_EXPERT_STRIDE = 1 << CODE_EXPERT_SHIFT
NT_DIMS = (((1,), (1,)), ((), ()))


def _cparams(*sem, fuse_inputs=None):
    return pltpu.CompilerParams(dimension_semantics=sem, vmem_limit_bytes=VMEM_LIMIT_BYTES,
                                allow_input_fusion=fuse_inputs)


def _layer_norm(x, g, b):
    mu = jnp.mean(x, axis=-1, keepdims=True)
    xc = x - mu
    var = jnp.mean(xc * xc, axis=-1, keepdims=True)
    return xc * lax.rsqrt(var + LN_EPS) * g + b


def _rms_norm(x, g):
    return x * lax.rsqrt(jnp.mean(x * x, axis=-1, keepdims=True) + RMS_EPS) * g


def _rope(x, cos, sin_signed, half):
    w = x.shape[-1]
    lane = lax.broadcasted_iota(jnp.int32, (1, w), 1)
    first = (lane & (2 * half - 1)) < half
    partner = jnp.where(first, pltpu.roll(x, w - half, 1), pltpu.roll(x, half, 1))
    return x * cos + partner * sin_signed


def _segment_mean(x, bd):
    hi = x.astype(BF16)
    lo = (x - hi.astype(F32)).astype(BF16)
    return (jnp.dot(hi, bd, preferred_element_type=F32) + jnp.dot(lo, bd, preferred_element_type=F32))


def _swap_lane_halves(x):
    return pltpu.roll(x, x.shape[-1] // 2, 1)


def _in_proj_kernel(x_ref, w_ref, *out_refs, splits):
    xb = x_ref[...].astype(BF16)
    for o_ref, (c0, c1) in zip(out_refs, splits):
        o_ref[...] = jnp.dot(xb, w_ref[:, c0:c1], preferred_element_type=F32).astype(o_ref.dtype)


def _in_proj(x2d, w_bf, splits, tm):
    n, d = x2d.shape
    widths = [c1 - c0 for c0, c1 in splits]
    return pl.pallas_call(
        functools.partial(_in_proj_kernel, splits=splits),
        grid=(n // tm,),
        in_specs=[pl.BlockSpec((tm, d), lambda i: (i, 0)),
                  pl.BlockSpec(w_bf.shape, lambda i: (0, 0))],
        out_specs=[pl.BlockSpec((tm, w), lambda i: (i, 0)) for w in widths],
        out_shape=[jax.ShapeDtypeStruct((n, w), BF16) for w in widths],
        compiler_params=_cparams("parallel", fuse_inputs=[False, True]),
        name="in_proj",
    )(x2d, w_bf)


def _attn_a_kernel(q_ref, k_ref, v_ref, cq_ref, sq_ref, ck_ref, sk_ref, gq_ref, gk_ref, bd_ref,
                   o_ref, kn_ref, va_ref, *, hd):
    low = lax.broadcasted_iota(jnp.int32, (1, 2 * hd), 1) < hd

    @pl.when(pl.program_id(1) == 0)
    def _():
        k = k_ref[...].astype(F32)
        k = k * lax.rsqrt(_segment_mean(k * k, bd_ref[...]) + RMS_EPS) * gk_ref[...]
        k = _rope(k, ck_ref[...], sk_ref[...], hd // 4)
        k_sw = _swap_lane_halves(k)
        v = v_ref[...].astype(F32)
        kn_ref[0] = jnp.where(low, k, k_sw).astype(BF16)
        kn_ref[1] = jnp.where(low, k_sw, k).astype(BF16)
        va_ref[0] = jnp.where(low, v, 1.0).astype(BF16)
        va_ref[1] = jnp.where(low, _swap_lane_halves(v), 1.0).astype(BF16)

    scale = hd ** -0.5 * LOG2E

    n_pairs = A_HEADS // 2
    qs = []
    for pair in range(n_pairs):
        q = q_ref[:, pair * 2 * hd:(pair + 1) * 2 * hd].astype(F32)
        q = q * lax.rsqrt(_segment_mean(q * q, bd_ref[...]) + RMS_EPS) * gq_ref[...]
        q = _rope(q, cq_ref[...], sq_ref[...], hd // 4) * scale
        qs.append([jnp.where(keep, q, 0.0).astype(BF16) for keep in (low, jnp.logical_not(low))])

    def scores(pair):
        kv = (2 * pair) // A_GROUP
        return [lax.dot_general(qh, kn_ref[kv], NT_DIMS, preferred_element_type=F32) for qh in qs[pair]]

    outs = []
    s_next = scores(0)
    for pair in range(n_pairs):
        s_cur = s_next
        if pair + 1 < n_pairs:
            s_next = scores(pair + 1)
        kv = (2 * pair) // A_GROUP
        o_pair = []
        for s in s_cur:
            e = jnp.exp2(s - jnp.max(s, axis=-1, keepdims=True)).astype(BF16)
            oa = jnp.dot(e, va_ref[kv], preferred_element_type=F32)
            l = jnp.where(low, _swap_lane_halves(oa), oa)
            o_pair.append(oa * (1.0 / l))
        outs.append(jnp.where(low, o_pair[0], _swap_lane_halves(o_pair[1])))
    o_ref[...] = jnp.concatenate(outs, axis=-1).astype(o_ref.dtype)


def _attn_a(qa, ka, va, cos, sin, gq, gk, bd, batch, seq, tq):
    n, dq = qa.shape
    hd = dq // A_HEADS
    assert 2 * hd == LANES and ka.shape[1] == LANES and va.shape[1] == LANES
    nq = seq // tq
    row = lambda b, i: (b * nq + i, 0)
    per_b = lambda b, i: (b, 0)
    const = lambda b, i: (0, 0)
    return pl.pallas_call(
        functools.partial(_attn_a_kernel, hd=hd),
        grid=(batch, nq),
        in_specs=[pl.BlockSpec((tq, dq), row),
                  pl.BlockSpec((seq, LANES), per_b),
                  pl.BlockSpec((seq, LANES), per_b),
                  pl.BlockSpec((tq, LANES), lambda b, i: (i, 0)),
                  pl.BlockSpec((tq, LANES), lambda b, i: (i, 0)),
                  pl.BlockSpec((seq, LANES), const),
                  pl.BlockSpec((seq, LANES), const),
                  pl.BlockSpec((1, LANES), const),
                  pl.BlockSpec((1, LANES), const),
                  pl.BlockSpec((LANES, LANES), const)],
        out_specs=pl.BlockSpec((tq, dq), row),
        out_shape=jax.ShapeDtypeStruct((n, dq), BF16),
        scratch_shapes=[pltpu.VMEM((A_KV_HEADS, seq, LANES), BF16),
                        pltpu.VMEM((A_KV_HEADS, seq, LANES), BF16)],
        compiler_params=_cparams("arbitrary", "arbitrary"),
        name="attn_a",
    )(qa, ka, va, cos, sin, cos, sin, gq, gk, bd)


def _attn_b_kernel(q_ref, k_ref, v_ref, cq_ref, sq_ref, ck_ref, sk_ref,
                   lq1_ref, lk1_ref, lq2_ref, lk2_ref, g_ref, o_ref, kr_ref, va_ref, *, hd, lambda_init):
    low = lax.broadcasted_iota(jnp.int32, (1, 2 * hd), 1) < hd
    vd = 2 * hd

    @pl.when(pl.program_id(1) == 0)
    def _():
        ones = jnp.ones((v_ref.shape[0], vd), BF16)
        for h in range(B_HEADS):
            k = _rope(k_ref[:, h * 2 * hd:(h + 1) * 2 * hd].astype(F32), ck_ref[...], sk_ref[...], hd // 2)
            k_sw = _swap_lane_halves(k)
            kr_ref[2 * h] = jnp.where(low, k, k_sw).astype(BF16)
            kr_ref[2 * h + 1] = jnp.where(low, k_sw, k).astype(BF16)
            va_ref[h] = jnp.concatenate([v_ref[:, h * vd:(h + 1) * vd], ones], axis=-1)

    lam = (jnp.exp(jnp.sum(lq1_ref[...] * lk1_ref[...], axis=-1, keepdims=True))
           - jnp.exp(jnp.sum(lq2_ref[...] * lk2_ref[...], axis=-1, keepdims=True))
           + lambda_init)
    scale = hd ** -0.5 * LOG2E
    qs = []
    for h in range(B_HEADS):
        q = q_ref[:, h * 2 * hd:(h + 1) * 2 * hd].astype(F32)
        q = _rope(q, cq_ref[...], sq_ref[...], hd // 2) * scale
        qs.append([jnp.where(keep, q, 0.0).astype(BF16) for keep in (low, jnp.logical_not(low))])

    def scores(h):
        return [lax.dot_general(qm, kr_ref[2 * h + c], NT_DIMS, preferred_element_type=F32)
                for c, qm in enumerate(qs[h])]

    outs = []
    s_next = scores(0)
    for h in range(B_HEADS):
        s_cur = s_next
        if h + 1 < B_HEADS:
            s_next = scores(h + 1)
        oa = [jnp.dot(jnp.exp2(s - jnp.max(s, axis=-1, keepdims=True)).astype(BF16), va_ref[h],
                      preferred_element_type=F32) for s in s_cur]
        o = oa[0][:, :vd] * (1.0 / oa[0][:, vd:]) - oa[1][:, :vd] * (lam / oa[1][:, vd:])
        outs.append(_rms_norm(o, g_ref[...]) * (1.0 - lambda_init))
    o_ref[...] = jnp.concatenate(outs, axis=-1).astype(o_ref.dtype)


def _attn_b(qb, kb, vb, cos, sin, lq1, lk1, lq2, lk2, subln, batch, seq, tq, lambda_init):
    n, dq = qb.shape
    hd = dq // (2 * B_HEADS)
    assert 2 * hd == LANES and vb.shape[1] == B_HEADS * LANES
    nq = seq // tq
    row = lambda b, i: (b * nq + i, 0)
    per_b = lambda b, i: (b, 0)
    const = lambda b, i: (0, 0)
    vec = pl.BlockSpec((1, hd), const)
    return pl.pallas_call(
        functools.partial(_attn_b_kernel, hd=hd, lambda_init=lambda_init),
        grid=(batch, nq),
        in_specs=[pl.BlockSpec((tq, dq), row),
                  pl.BlockSpec((seq, dq), per_b),
                  pl.BlockSpec((seq, vb.shape[1]), per_b),
                  pl.BlockSpec((tq, LANES), lambda b, i: (i, 0)),
                  pl.BlockSpec((tq, LANES), lambda b, i: (i, 0)),
                  pl.BlockSpec((seq, LANES), const),
                  pl.BlockSpec((seq, LANES), const),
                  vec, vec, vec, vec,
                  pl.BlockSpec((1, LANES), const)],
        out_specs=pl.BlockSpec((tq, vb.shape[1]), row),
        out_shape=jax.ShapeDtypeStruct((n, vb.shape[1]), BF16),
        scratch_shapes=[pltpu.VMEM((2 * B_HEADS, seq, LANES), BF16),
                        pltpu.VMEM((B_HEADS, seq, 2 * LANES), BF16)],
        compiler_params=_cparams("arbitrary", "arbitrary"),
        name="attn_b",
    )(qb, kb, vb, cos, sin, cos, sin, lq1, lk1, lq2, lk2, subln)


def _matmul_kernel(x_ref, w_ref, o_ref):
    o_ref[...] = jnp.dot(x_ref[...].astype(BF16), w_ref[...],
                         preferred_element_type=F32).astype(o_ref.dtype)


def _mem_kv(mem2d, w_bf, tm):
    n, d = mem2d.shape
    return pl.pallas_call(
        _matmul_kernel,
        grid=(n // tm,),
        in_specs=[pl.BlockSpec((tm, d), lambda i: (i, 0)),
                  pl.BlockSpec(w_bf.shape, lambda i: (0, 0))],
        out_specs=pl.BlockSpec((tm, w_bf.shape[1]), lambda i: (i, 0)),
        out_shape=jax.ShapeDtypeStruct((n, w_bf.shape[1]), BF16),
        compiler_params=_cparams("parallel", fuse_inputs=[False, True]),
        name="mem_kv",
    )(mem2d, w_bf)


def _mix_xattn_kernel(oa_ref, ob_ref, x_ref, wmix_ref, g1_ref, b1_ref, wq_ref, kv_ref, wo_ref,
                      g2_ref, b2_ref, wr_ref, br_ref,
                      x2t_ref, code_ref, gate_ref, cnt_ref, carry_ref, *, alpha, n_experts, sub):
    @pl.when(pl.program_id(0) == 0)
    def _():
        carry_ref[...] = jnp.zeros(carry_ref.shape, carry_ref.dtype)

    da = oa_ref.shape[1]
    d = x_ref.shape[1]
    hd = d // MEM_HEADS
    scale = hd ** -0.5
    starts = list(range(0, x_ref.shape[0], sub))
    subs = [pl.ds(r0, sub) for r0 in starts]

    mix = [jnp.dot(oa_ref[rows, :], wmix_ref[:da, :], preferred_element_type=F32)
           + jnp.dot(ob_ref[rows, :], wmix_ref[da:, :], preferred_element_type=F32) for rows in subs]
    x1 = [_layer_norm(alpha * x_ref[rows, :] + m, g1_ref[...], b1_ref[...]) for rows, m in zip(subs, mix)]
    q = [jnp.dot(v.astype(BF16), wq_ref[...], preferred_element_type=F32).astype(BF16) for v in x1]

    def mem_attention(qv):
        outs = []
        for h in range(MEM_HEADS):
            s = lax.dot_general(qv[:, h * hd:(h + 1) * hd], kv_ref[:, h * hd:(h + 1) * hd],
                                NT_DIMS, preferred_element_type=F32) * scale
            e = jnp.exp(s - jnp.max(s, axis=-1, keepdims=True))
            l = jnp.sum(e, axis=-1, keepdims=True)
            o = jnp.dot(e.astype(BF16), kv_ref[:, d + h * hd:d + (h + 1) * hd],
                        preferred_element_type=F32)
            outs.append((o * (1.0 / l)).astype(BF16))
        return jnp.concatenate(outs, axis=-1)

    att = [mem_attention(qv) for qv in q]
    xa = [jnp.dot(a, wo_ref[...], preferred_element_type=F32) for a in att]
    x2 = [_layer_norm(alpha * v + a, g2_ref[...], b2_ref[...]) for v, a in zip(x1, xa)]
    for r0, v in zip(starts, x2):
        for j in range(SUBLANES):
            x2t_ref[pl.ds(r0 * SUBLANES + j, sub, stride=SUBLANES), :] = v[:, j * LANES:(j + 1) * LANES]
    all_logits = [jnp.dot(v.astype(BF16), wr_ref[...], preferred_element_type=F32) + br_ref[...] for v in x2]

    tm = x_ref.shape[0]
    lt = jnp.concatenate(all_logits, axis=0).T[:n_experts, :]
    row = lax.broadcasted_iota(jnp.int32, (n_experts, tm), 0).astype(F32)
    vals, idxs = [], []
    for _ in range(TOP_K):
        m = jnp.max(lt, axis=0, keepdims=True)
        idx = jnp.min(jnp.where(lt == m, row, float(n_experts)), axis=0, keepdims=True)
        vals.append(m)
        idxs.append(idx)
        lt = jnp.where(row == idx, -jnp.inf, lt)
    es = [jnp.exp(v - vals[0]) for v in vals]
    den = es[0] + es[1] + es[2] + es[3]

    chosen = [row == idx for idx in idxs]
    cnt = sum(c.astype(F32) for c in chosen)
    r_id = lax.broadcasted_iota(jnp.int32, (sub, sub), 0)
    c_id = lax.broadcasted_iota(jnp.int32, (sub, sub), 1)
    upper = jnp.where(r_id < c_id, 1.0, 0.0).astype(BF16)
    carry = carry_ref[...]
    before = []
    for r0 in starts:
        cnt_s = cnt[:, r0:r0 + sub]
        before.append(carry + jnp.dot(cnt_s.astype(BF16), upper, preferred_element_type=F32))
        carry = carry + jnp.sum(cnt_s, axis=1, keepdims=True)
    carry_ref[...] = carry
    cnt_ref[...] = carry
    before = jnp.concatenate(before, axis=1)

    srow = lax.broadcasted_iota(jnp.int32, (SUBLANES, tm), 0)
    grow = lax.broadcasted_iota(jnp.int32, (LANES, tm), 0)
    code_out = jnp.zeros((SUBLANES, tm), jnp.int32)
    gate_out = jnp.zeros((LANES, tm), F32)
    for k in range(TOP_K):
        rank = jnp.sum(jnp.where(chosen[k], before, 0.0), axis=0, keepdims=True)
        code = idxs[k].astype(jnp.int32) * CODE_EXPERT_STRIDE + rank.astype(jnp.int32)
        code_out = jnp.where(srow == k, code, code_out)
        gate_out = jnp.where(grow == k, es[k] / den, gate_out)
    code_ref[...] = code_out
    gate_ref[...] = gate_out.T


def _mix_xattn(oa, ob, x2d, wmix, g1, b1, wq, kvmem, wo, g2, b2, wr, br, seq, mem_tokens, tm, alpha):
    n, d = x2d.shape
    tiles_per_b = seq // tm
    row = lambda i: (i, 0)
    const = lambda i: (0, 0)
    full = lambda a: pl.BlockSpec(a.shape, const)
    n_experts = wr.shape[1]
    assert d == SUBLANES * LANES and n < CODE_EXPERT_STRIDE and n_experts <= LANES
    wr = jnp.pad(wr, ((0, 0), (0, LANES - n_experts)))
    br = jnp.pad(br, ((0, 0), (0, LANES - n_experts)), constant_values=-jnp.inf)
    return pl.pallas_call(
        functools.partial(_mix_xattn_kernel, alpha=alpha, n_experts=n_experts, sub=_pick_tile(tm, 256)),
        grid=(n // tm,),
        in_specs=[pl.BlockSpec((tm, oa.shape[1]), row),
                  pl.BlockSpec((tm, ob.shape[1]), row),
                  pl.BlockSpec((tm, d), row),
                  full(wmix), full(g1), full(b1), full(wq),
                  pl.BlockSpec((mem_tokens, kvmem.shape[1]), lambda i: (i // tiles_per_b, 0)),
                  full(wo), full(g2), full(b2), full(wr), full(br)],
        out_specs=[pl.BlockSpec((tm * SUBLANES, LANES), row),
                   pl.BlockSpec((SUBLANES, tm), lambda i: (0, i)), pl.BlockSpec((tm, LANES), row),
                   pl.BlockSpec((n_experts, 1), const)],
        out_shape=[jax.ShapeDtypeStruct((n * SUBLANES, LANES), F32),
                   jax.ShapeDtypeStruct((SUBLANES, n), jnp.int32),
                   jax.ShapeDtypeStruct((n, LANES), F32),
                   jax.ShapeDtypeStruct((n_experts, 1), F32)],
        scratch_shapes=[pltpu.VMEM((n_experts, 1), F32)],
        compiler_params=_cparams("arbitrary", fuse_inputs=[k in (3, 6, 8, 11) for k in range(13)]),
        name="mix_xattn",
    )(oa, ob, x2d, wmix, g1, b1, wq, kvmem, wo, g2, b2, wr, br)


def _tile_map(counts, code4, tg, n_tiles):
    n_experts = counts.shape[0]
    experts = jnp.arange(n_experts, dtype=jnp.int32)
    padded = ((counts + tg - 1) // tg) * tg
    pad_end = jnp.cumsum(padded)
    pad_start = pad_end - padded
    n_used = pad_end[-1] // tg

    pair_expert = lax.shift_right_logical(code4, CODE_EXPERT_SHIFT)
    pair_start = jnp.sum(jnp.where(pair_expert[..., None] == experts, pad_start, 0), axis=-1)
    slots = pair_start + (code4 & (CODE_EXPERT_STRIDE - 1))

    tile_id = jnp.arange(n_tiles, dtype=jnp.int32)
    tile_start = tile_id * tg
    tile_expert = jnp.minimum(
        jnp.sum((tile_start[:, None] >= pad_end[None, :]).astype(jnp.int32), axis=1), n_experts - 1)
    onehot = (tile_expert[:, None] == experts[None, :]).astype(jnp.int32)
    pick = lambda per_expert: jnp.sum(onehot * per_expert[None, :], axis=1)
    tile_valid = jnp.clip(pick(pad_start + counts) - tile_start, 0, tg)
    next_tile = pick(pad_end) // tg
    has_next = (next_tile < n_used).astype(jnp.int32)
    next_expert = jnp.sum((tile_id[:, None] == jnp.minimum(next_tile, n_tiles - 1)[None, :]).astype(jnp.int32)
                          * tile_expert[:, None], axis=0)
    nonempty_before = jnp.cumsum((counts > 0).astype(jnp.int32)) - (counts > 0).astype(jnp.int32)
    tile_parity = pick(nonempty_before) & 1
    i32 = lambda a: a.astype(jnp.int32)
    return (i32(slots), i32(tile_expert), i32(tile_valid), i32(has_next), i32(next_expert),
            i32(tile_parity), i32(n_used).reshape(1))


def _token_tile(ref, index):
    if isinstance(index, int):
        return ref.at[pl.ds(index * SUBLANES, SUBLANES)]
    return ref.at[pl.ds(pl.multiple_of(index * SUBLANES, SUBLANES), SUBLANES)]


def _wait_tiles(src_ref, dst_ref, sem, count):
    def wait(_, c):
        pltpu.make_async_copy(_token_tile(src_ref, 0), _token_tile(dst_ref, 0), sem).wait()
        return c
    lax.fori_loop(0, count, wait, 0, unroll=8)


def _load_token_rows(ref, rows):
    return jnp.concatenate([ref[pl.ds(j, rows, stride=SUBLANES), :] for j in range(SUBLANES)], axis=-1)


def _dispatch_kernel(slot_ref, x_ref, o_hbm, sem, *, tokens):
    def start(t, c):
        for k in range(TOP_K):
            pltpu.make_async_copy(_token_tile(x_ref, t), _token_tile(o_hbm, slot_ref[k, t]),
                                  sem).start(priority=k % 2)
        return c
    lax.fori_loop(0, tokens, start, 0, unroll=2)
    _wait_tiles(x_ref, o_hbm, sem, tokens * TOP_K)


def _dispatch(x2t, slots, n_slots, tm):
    n = x2t.shape[0] // SUBLANES
    nt = n // tm
    return pl.pallas_call(
        functools.partial(_dispatch_kernel, tokens=tm),
        grid=(nt,),
        in_specs=[pl.BlockSpec((SUBLANES, tm), lambda i: (0, i), memory_space=pltpu.SMEM),
                  pl.BlockSpec((tm * SUBLANES, LANES), lambda i: (i, 0))],
        out_specs=pl.BlockSpec(memory_space=pl.ANY),
        out_shape=jax.ShapeDtypeStruct((n_slots * SUBLANES, LANES), x2t.dtype),
        scratch_shapes=[pltpu.SemaphoreType.DMA(())],
        compiler_params=_cparams("arbitrary"),
        name="dispatch",
    )(slots, x2t)


def _experts_kernel(te_ref, tv_ref, hn_ref, ne_ref, par_ref, nused_ref,
                    x_ref, wg_hbm, bg_ref, wu_hbm, bu_ref, wd_hbm, bd_ref,
                    o_ref, wg_st, wu_st, wd_st, wgb_ref, wub_ref, wdb_ref, sem):
    i = pl.program_id(0)
    used = i < nused_ref[0]
    first = used & ((i == 0) | (te_ref[i] != te_ref[jnp.maximum(i - 1, 0)]))
    tg = x_ref.shape[0] // SUBLANES

    def weight_copies(expert, slot):
        return [pltpu.make_async_copy(hbm.at[expert], st.at[slot], sem.at[slot])
                for hbm, st in ((wg_hbm, wg_st), (wu_hbm, wu_st), (wd_hbm, wd_st))]

    @pl.when(used & (i == 0))
    def _():
        for c in weight_copies(te_ref[0], par_ref[0]):
            c.start()

    @pl.when(first)
    def _():
        slot = par_ref[i]
        for c in weight_copies(te_ref[i], slot):
            c.wait()

        @pl.when(hn_ref[i] == 1)
        def _():
            for c in weight_copies(ne_ref[i], 1 - slot):
                c.start()

        wgb_ref[...] = wg_st[slot].astype(BF16)
        wub_ref[...] = wu_st[slot].astype(BF16)
        wdb_ref[...] = wd_st[slot].astype(BF16)

    def expert_rows(rows):
        row = lax.broadcasted_iota(jnp.int32, (rows, 1), 0)
        x = jnp.where(row < tv_ref[i], _load_token_rows(x_ref, rows), 0.0).astype(BF16)
        g = jnp.dot(x, wgb_ref[...], preferred_element_type=F32) + bg_ref[0]
        u = jnp.dot(x, wub_ref[...], preferred_element_type=F32) + bu_ref[0]
        g = jnp.minimum(g, SWIGLU_LIMIT)
        u = jnp.clip(u, -SWIGLU_LIMIT, SWIGLU_LIMIT)
        h = g * (1.0 / (1.0 + jnp.exp(-SWIGLU_ALPHA * g))) * (u + 1.0)
        y = jnp.dot(h.astype(BF16), wdb_ref[...], preferred_element_type=F32) + bd_ref[0]
        for j in range(SUBLANES):
            o_ref[pl.ds(j, rows, stride=SUBLANES), :] = y[:, j * LANES:(j + 1) * LANES]
        if rows < tg:
            o_ref[rows * SUBLANES:, :] = jnp.zeros(((tg - rows) * SUBLANES, LANES), o_ref.dtype)

    half = tg // 2
    pl.when(used & (tv_ref[i] > half))(lambda: expert_rows(tg))
    pl.when(used & (tv_ref[i] <= half))(lambda: expert_rows(half))

    @pl.when(jnp.logical_not(used))
    def _():
        o_ref[...] = jnp.zeros(o_ref.shape, o_ref.dtype)


def _experts(xs, tile_expert, tile_valid, has_next, next_expert, tile_parity, n_used,
             wg, bg, wu, bu, wd, bd, tg):
    e, d, f = wg.shape
    n_tiles = xs.shape[0] // (tg * SUBLANES)
    xrow = lambda i, te, tv, hn, ne, par, nu: (jnp.minimum(i, nu[0] - 1), 0)
    bmap = lambda i, te, tv, hn, ne, par, nu: (te[i], 0, 0)
    hbm = pl.BlockSpec(memory_space=pl.ANY)
    grid_spec = pltpu.PrefetchScalarGridSpec(
        num_scalar_prefetch=6,
        grid=(n_tiles,),
        in_specs=[pl.BlockSpec((tg * SUBLANES, LANES), xrow),
                  hbm, pl.BlockSpec((1, 1, f), bmap),
                  hbm, pl.BlockSpec((1, 1, f), bmap),
                  hbm, pl.BlockSpec((1, 1, d), bmap)],
        out_specs=pl.BlockSpec((tg * SUBLANES, LANES), lambda i, te, tv, hn, ne, par, nu: (i, 0)),
        scratch_shapes=[pltpu.VMEM((2, d, f), F32), pltpu.VMEM((2, d, f), F32), pltpu.VMEM((2, f, d), F32),
                        pltpu.VMEM((d, f), BF16), pltpu.VMEM((d, f), BF16), pltpu.VMEM((f, d), BF16),
                        pltpu.SemaphoreType.DMA((2,))],
    )
    return pl.pallas_call(
        _experts_kernel,
        grid_spec=grid_spec,
        out_shape=jax.ShapeDtypeStruct(xs.shape, F32),
        compiler_params=_cparams("arbitrary"),
        name="experts",
    )(tile_expert, tile_valid, has_next, next_expert, tile_parity, n_used,
      xs, wg, bg.reshape(e, 1, f), wu, bu.reshape(e, 1, f), wd, bd.reshape(e, 1, d))


def _combine_kernel(cur_ref, nxt_ref, ys_hbm, x2t_ref, gate_ref, g_ref, b_ref, o_ref, buf0_ref, buf1_ref, sem,
                    *, alpha, n_steps):
    i = pl.program_id(0)
    tc = gate_ref.shape[0]
    n_copies = tc * TOP_K
    bufs = (buf0_ref, buf1_ref)

    def copy(slot_ref, b, t, k):
        return pltpu.make_async_copy(_token_tile(ys_hbm, slot_ref[k, t]),
                                     _token_tile(bufs[b].at[k], t), sem.at[b])

    @pl.when(i == 0)
    def _():
        def start(t, c):
            for k in range(TOP_K):
                copy(cur_ref, 0, t, k).start(priority=k % 2)
            return c
        lax.fori_loop(0, tc, start, 0, unroll=2)

    for b in range(2):
        @pl.when(i % 2 == b)
        def _(b=b):
            _wait_tiles(ys_hbm, bufs[b].at[0], sem.at[b], n_copies)
            for t in range(tc):
                for k in range(TOP_K):
                    copy(nxt_ref, 1 - b, t, k).start(priority=k % 2)
            gates = gate_ref[...]
            y = alpha * _load_token_rows(x2t_ref, tc)
            for k in range(TOP_K):
                y = y + gates[:, k:k + 1] * _load_token_rows(bufs[b].at[k], tc)
            o_ref[...] = _layer_norm(y, g_ref[...], b_ref[...])

            @pl.when(i == n_steps - 1)
            def _():
                _wait_tiles(ys_hbm, bufs[1 - b].at[0], sem.at[1 - b], n_copies)


def _combine(ys, slots, x2t, gates, g3, b3, tc, alpha):
    n = x2t.shape[0] // SUBLANES
    d = SUBLANES * LANES
    nt = n // tc
    const = lambda i: (0, 0)
    return pl.pallas_call(
        functools.partial(_combine_kernel, alpha=alpha, n_steps=nt),
        grid=(nt,),
        in_specs=[pl.BlockSpec((SUBLANES, tc), lambda i: (0, i), memory_space=pltpu.SMEM),
                  pl.BlockSpec((SUBLANES, tc), lambda i: (0, jnp.minimum(i + 1, nt - 1)),
                               memory_space=pltpu.SMEM),
                  pl.BlockSpec(memory_space=pl.ANY),
                  pl.BlockSpec((tc * SUBLANES, LANES), lambda i: (i, 0)),
                  pl.BlockSpec((tc, LANES), lambda i: (i, 0)),
                  pl.BlockSpec(g3.shape, const), pl.BlockSpec(b3.shape, const)],
        out_specs=pl.BlockSpec((tc, d), lambda i: (i, 0)),
        out_shape=jax.ShapeDtypeStruct((n, d), F32),
        scratch_shapes=[pltpu.VMEM((TOP_K, tc * SUBLANES, LANES), ys.dtype),
                        pltpu.VMEM((TOP_K, tc * SUBLANES, LANES), ys.dtype),
                        pltpu.SemaphoreType.DMA((2,))],
        compiler_params=_cparams("arbitrary"),
        name="combine",
    )(slots, slots, ys, x2t, gates, g3, b3)


def _rope_tables(seq, a_hd, b_hd):
    def cos_sin(pos, dim):
        inv = ROPE_THETA ** (-np.arange(0, dim, 2, dtype=np.float64) / dim)
        ang = pos.astype(np.float64)[:, None] * inv[None, :]
        return np.cos(ang).astype(np.float32), np.sin(ang).astype(np.float32)

    t = np.arange(seq, dtype=np.int32)
    cr, sr = cos_sin(t // GRID_W, a_hd // 2)
    cc, sc = cos_sin(t % GRID_W, a_hd // 2)
    cs, ss = cos_sin(t, b_hd)
    twice = lambda a: np.concatenate([a, a], axis=-1)
    cos_a = twice(np.concatenate([cr, cr, cc, cc], axis=-1))
    sin_a = twice(np.concatenate([-sr, sr, -sc, sc], axis=-1))
    cos_b = twice(np.concatenate([cs, cs], axis=-1))
    sin_b = twice(np.concatenate([-ss, ss], axis=-1))
    return tuple(jnp.asarray(a) for a in (cos_a, sin_a, cos_b, sin_b))


def _pick_tile(n, pref):
    t = min(pref, n)
    while n % t:
        t //= 2
    return t


def kernel(x, mem, w_in, a_q_norm, a_k_norm, b_lambda_q1, b_lambda_k1, b_lambda_q2, b_lambda_k2, b_subln, w_mix_out, ln1_g, ln1_b, w_mem_q, w_mem_kv, w_mem_out, ln2_g, ln2_b, w_router, b_router, w_e_gate, b_e_gate, w_e_up, b_e_up, w_e_down, b_e_down, ln3_g, ln3_b):
    batch, seq, d = x.shape
    depth = w_in.shape[0]
    n = batch * seq
    mem_tokens = mem.shape[1]
    n_experts = w_router.shape[-1]
    a_hd = a_q_norm.shape[-1]
    b_hd = b_lambda_q1.shape[-1]
    alpha = (2.0 * depth) ** 0.25

    a_q, a_kv = A_HEADS * a_hd, A_KV_HEADS * a_hd
    b_qk, b_v = B_HEADS * 2 * b_hd, B_HEADS * 2 * b_hd
    edges = [0, a_q, a_q + a_kv, a_q + 2 * a_kv, a_q + 2 * a_kv + b_qk,
             a_q + 2 * a_kv + 2 * b_qk, a_q + 2 * a_kv + 2 * b_qk + b_v]
    splits = tuple(zip(edges[:-1], edges[1:]))
    assert edges[-1] == w_in.shape[-1]

    tm = _pick_tile(seq, 1024)
    tq = _pick_tile(seq, 512)
    tg = 512
    tc = _pick_tile(n, 256)
    n_tiles = (n * TOP_K) // tg + n_experts
    cos_a, sin_a, cos_b, sin_b = _rope_tables(seq, a_hd, b_hd)
    seg = jnp.arange(LANES, dtype=jnp.int32) // a_hd
    bd = jnp.where(seg[:, None] == seg[None, :], 1.0 / a_hd, 0.0).astype(BF16)
    row2 = lambda v: v.reshape(1, -1)
    twice = lambda v: jnp.concatenate([v, v]).reshape(1, -1)

    h = x.reshape(n, d)
    mem2d = mem.reshape(batch * mem_tokens, d)
    for layer in range(depth):
        lambda_init = 0.8 - 0.6 * math.exp(-0.3 * layer)
        qa, ka, va, qb, kb, vb = _in_proj(h, w_in[layer].astype(BF16), splits, tm)
        oa = _attn_a(qa, ka, va, cos_a, sin_a, twice(a_q_norm[layer]), twice(a_k_norm[layer]), bd,
                     batch, seq, tq)
        ob = _attn_b(qb, kb, vb, cos_b, sin_b, row2(b_lambda_q1[layer]), row2(b_lambda_k1[layer]),
                     row2(b_lambda_q2[layer]), row2(b_lambda_k2[layer]), row2(b_subln[layer]),
                     batch, seq, tq, lambda_init)
        kvmem = _mem_kv(mem2d, w_mem_kv[layer].astype(BF16), _pick_tile(batch * mem_tokens, 512))
        x2t, code, gates, counts = _mix_xattn(
            oa, ob, h, w_mix_out[layer].astype(BF16), row2(ln1_g[layer]), row2(ln1_b[layer]),
            w_mem_q[layer].astype(BF16), kvmem, w_mem_out[layer].astype(BF16),
            row2(ln2_g[layer]), row2(ln2_b[layer]), w_router[layer].astype(BF16),
            row2(b_router[layer]), seq, mem_tokens, tm, alpha)
        slots, tile_expert, tile_valid, has_next, next_expert, tile_parity, n_used = _tile_map(
            counts.reshape(-1).astype(jnp.int32), code, tg, n_tiles)
        xs = _dispatch(x2t, slots, n_tiles * tg, _pick_tile(seq, 512))
        ys = _experts(xs, tile_expert, tile_valid, has_next, next_expert, tile_parity, n_used,
                      w_e_gate[layer], b_e_gate[layer], w_e_up[layer], b_e_up[layer],
                      w_e_down[layer], b_e_down[layer], tg)
        h = _combine(ys, slots, x2t, gates, row2(ln3_g[layer]), row2(ln3_b[layer]), tc, alpha)
    return h.reshape(batch, seq, d)
```
